```python
import math
import jax
import jax.numpy as jnp
from jax import lax
import numpy as np

D_MODEL = 1024
BATCH = 2
SEQ = 8192
DEPTH = 4
DEC_BATCH = 128
DEC_SEQ = 1
PAST_LEN = 2048
PAGE_SIZE = 128

NSA_HEADS = 8
NSA_KV_HEADS = 2
NSA_GROUP = NSA_HEADS // NSA_KV_HEADS
NSA_HD = 64
CMP_BLOCK = 64
CMP_HIDDEN = 256
N_SELECT = 16
WINDOW = 512
DIFF_HEADS = 4
DIFF_QK = 64
DIFF_VD = 2 * DIFF_QK
N_BUCKETS = 32
MAX_DISTANCE = 128
N_ALL_HEADS = NSA_HEADS + DIFF_HEADS
D_FF = 4 * D_MODEL
Q_BLOCK = 128
EPS = 1e-6
NEG = -1e30

NSA_W = NSA_HEADS * NSA_HD
DIFF_W = DIFF_HEADS * DIFF_VD
KV_W = NSA_KV_HEADS * NSA_HD
COL_SIZES = (NSA_W, NSA_HEADS * 3, KV_W, KV_W, KV_W, KV_W, KV_W, KV_W,
             DIFF_HEADS * 2 * DIFF_QK, DIFF_HEADS * 2 * DIFF_QK, DIFF_HEADS * DIFF_VD)
D_IN = sum(COL_SIZES)

kernel_name = 'nsa_diffattn_parallel_hybrid_step'


def rmsnorm(x, g):
    xf = x.astype(jnp.float32)
    y = xf * lax.rsqrt(jnp.mean(xf * xf, axis=-1, keepdims=True) + EPS)
    return (y * g.astype(jnp.float32)).astype(x.dtype)


def t5_bucket(dist):
    n = jnp.maximum(dist, 0)
    max_exact = N_BUCKETS // 2
    nf = jnp.maximum(n, 1).astype(jnp.float32)
    large = max_exact + (jnp.log(nf / max_exact) / math.log(MAX_DISTANCE / max_exact)
                         * (N_BUCKETS - max_exact)).astype(jnp.int32)
    large = jnp.minimum(large, N_BUCKETS - 1)
    return jnp.where(n < max_exact, n, large)


def masked_softmax(s, mask):
    s = jnp.where(mask, s.astype(jnp.float32), NEG)
    m = jnp.max(s, axis=-1, keepdims=True)
    p = jnp.where(mask, jnp.exp(s - m), 0.0)
    den = jnp.sum(p, axis=-1, keepdims=True)
    return p / jnp.where(den > 0, den, 1.0)


def project(h, w_in_l):
    b, t = h.shape[:2]
    p = h @ w_in_l
    cuts = [int(c) for c in np.cumsum(COL_SIZES)[:-1]]
    nq, ng, ck, cv, sk, sv, wk, wv, dq, dk, dv = jnp.split(p, cuts, axis=-1)

    def kv(a, c):
        return jnp.stack([a.reshape(b, t, NSA_KV_HEADS, NSA_HD), c.reshape(b, t, NSA_KV_HEADS, NSA_HD)], axis=2)

    q_nsa = nq.reshape(b, t, NSA_KV_HEADS, NSA_GROUP, NSA_HD)
    gates = ng.reshape(b, t, NSA_KV_HEADS, NSA_GROUP, 3)
    q_diff = dq.reshape(b, t, DIFF_HEADS, 2, DIFF_QK)
    diff_kv = jnp.stack([dk.reshape(b, t, DIFF_HEADS, DIFF_VD), dv.reshape(b, t, DIFF_HEADS, DIFF_VD)], axis=2)
    return q_nsa, gates, kv(ck, cv), kv(sk, sv), kv(wk, wv), q_diff, diff_kv


def compress(rows, w1, w2, pe):
    b = rows.shape[0]
    nb = rows.shape[1] // CMP_BLOCK
    blk = rows.reshape(b, nb, CMP_BLOCK, 2, NSA_KV_HEADS, NSA_HD) + jnp.transpose(pe, (1, 0, 2))[None, None, :, :, None, :]
    flat = jnp.transpose(blk, (0, 1, 3, 4, 2, 5)).reshape(b, nb, 2, NSA_KV_HEADS, CMP_BLOCK * NSA_HD)
    hid = jax.nn.gelu(jnp.einsum('bncht,ctf->bnchf', flat, w1))
    out = jnp.einsum('bnchf,cfd->bnchd', hid, w2)
    return out[:, :, 0], out[:, :, 1]


def to_sel_blocks(rows):
    b, length = rows.shape[:2]
    nblk = -(-length // CMP_BLOCK)
    rows = jnp.pad(rows, ((0, 0), (0, nblk * CMP_BLOCK - length), (0, 0), (0, 0), (0, 0)))
    rows = rows.reshape(b, nblk, CMP_BLOCK, 2, NSA_KV_HEADS, NSA_HD)
    return jnp.transpose(rows, (0, 4, 1, 2, 3, 5))


def nsa_attend(q, gates, kc, vc, sel_blocks, kw, vw, q_pos, kw_pos, nsa_tab):
    b, nq = q.shape[:2]
    scale = NSA_HD ** -0.5
    nb = kc.shape[1]
    blk_idx = jnp.arange(nb, dtype=jnp.int32)
    cur = q_pos // CMP_BLOCK
    mask_c = blk_idx[None, :] < cur[:, None]
    dist_c = q_pos[:, None] - (blk_idx[None, :] * CMP_BLOCK + CMP_BLOCK - 1)
    bias_c = jnp.transpose(nsa_tab[t5_bucket(dist_c)], (2, 3, 0, 1))
    s_c = jnp.einsum('bqhgd,bnhd->bhgqn', q, kc) * scale + bias_c
    p_c = masked_softmax(s_c, mask_c)
    o_c = jnp.einsum('bhgqn,bnhd->bqhgd', p_c.astype(vc.dtype), vc)
    imp = jnp.where(mask_c, jnp.sum(p_c, axis=2), -1.0)
    n_top = min(N_SELECT - 1, nb)
    top_val, top_idx = lax.top_k(imp, n_top)
    lead = top_idx.shape[:3] + (1,)
    sel_idx = jnp.concatenate([top_idx, jnp.broadcast_to(cur[None, None, :, None], lead)], axis=-1)
    sel_ok = jnp.concatenate([top_val >= 0, jnp.ones(lead, dtype=bool)], axis=-1)
    ns = sel_idx.shape[-1]
    gathered = jax.vmap(jax.vmap(lambda blocks, idx: blocks[idx]))(sel_blocks, sel_idx)
    gathered = gathered.reshape(b, NSA_KV_HEADS, nq, ns * CMP_BLOCK, 2, NSA_HD)
    ks, vs = gathered[..., 0, :], gathered[..., 1, :]
    k_pos_s = (sel_idx[..., None] * CMP_BLOCK + jnp.arange(CMP_BLOCK, dtype=jnp.int32)).reshape(b, NSA_KV_HEADS, nq, ns * CMP_BLOCK)
    mask_s = jnp.repeat(sel_ok, CMP_BLOCK, axis=-1) & (k_pos_s <= q_pos[:, None])
    tab_t = jnp.transpose(nsa_tab, (1, 0, 2))
    kv_ar = jnp.arange(NSA_KV_HEADS)[None, :, None, None]
    bias_s = jnp.moveaxis(tab_t[kv_ar, t5_bucket(q_pos[:, None] - k_pos_s)], -1, 2)
    s_s = jnp.einsum('bqhgd,bhqkd->bhgqk', q, ks) * scale + bias_s
    p_s = masked_softmax(s_s, mask_s[:, :, None])
    o_s = jnp.einsum('bhgqk,bhqkd->bqhgd', p_s.astype(vs.dtype), vs)
    dist_w = q_pos[:, None] - kw_pos[None, :]
    mask_w = (dist_w >= 0) & (dist_w < WINDOW) & (kw_pos[None, :] >= 0)
    bias_w = jnp.transpose(nsa_tab[t5_bucket(dist_w)], (2, 3, 0, 1))
    s_w = jnp.einsum('bqhgd,bkhd->bhgqk', q, kw) * scale + bias_w
    p_w = masked_softmax(s_w, mask_w)
    o_w = jnp.einsum('bhgqk,bkhd->bqhgd', p_w.astype(vw.dtype), vw)
    g = jax.nn.sigmoid(gates.astype(jnp.float32)).astype(q.dtype)
    o = g[..., 0:1] * o_c + g[..., 1:2] * o_s + g[..., 2:3] * o_w
    return o.reshape(b, nq, NSA_W)


def diff_attend(q, kv, q_pos, k_pos, diff_tab, lam, lam_init, subln):
    b, nk = kv.shape[:2]
    k = kv[:, :, 0].reshape(b, nk, DIFF_HEADS, 2, DIFF_QK)
    v = kv[:, :, 1]
    dist = q_pos[:, None] - k_pos[None, :]
    mask = dist >= 0
    bias = jnp.transpose(diff_tab[t5_bucket(dist)], (2, 0, 1))
    s = jnp.einsum('bqhcd,bkhcd->bchqk', q, k) * DIFF_QK ** -0.5 + bias
    p = masked_softmax(s, mask)
    a = p[:, 0] - lam * p[:, 1]
    o = jnp.einsum('bhqk,bkhd->bqhd', a.astype(v.dtype), v)
    o = rmsnorm(o, subln) * (1.0 - lam_init)
    return o.reshape(b, q.shape[1], DIFF_W)


def sq_relu_mlp(h, w_up_l, w_down_l):
    return jnp.square(jax.nn.relu(h @ w_up_l)) @ w_down_l


def prompt_mixer(h, w_in_l, w_o_l, w_cmp1_l, w_cmp2_l, cmp_pe_l, lam, lam_init, subln_l, nsa_tab, diff_tab):
    b, s, _ = h.shape
    q, gates, cmp_kv, slc_kv, win_kv, dq, diff_kv = project(h, w_in_l)
    kc, vc = compress(cmp_kv[:, :(s // CMP_BLOCK) * CMP_BLOCK], w_cmp1_l, w_cmp2_l, cmp_pe_l)
    sel_blocks = to_sel_blocks(slc_kv)
    nqb = s // Q_BLOCK
    wb = WINDOW // Q_BLOCK
    wpad = jnp.pad(win_kv, ((0, 0), (WINDOW, 0), (0, 0), (0, 0), (0, 0)))
    wpad = wpad.reshape((b, nqb + wb, Q_BLOCK) + win_kv.shape[2:])
    bands = jnp.concatenate([wpad[:, i:i + nqb] for i in range(wb + 1)], axis=2)
    band_pos = (jnp.arange(nqb, dtype=jnp.int32)[:, None] * Q_BLOCK - WINDOW
                + jnp.arange((wb + 1) * Q_BLOCK, dtype=jnp.int32)[None, :])
    qpos_blocks = jnp.arange(s, dtype=jnp.int32).reshape(nqb, Q_BLOCK)

    def blocks(a):
        return jnp.swapaxes(a.reshape((b, nqb, Q_BLOCK) + a.shape[2:]), 0, 1)

    def nsa_step(args):
        qb, gb, band, bpos, qpos = args
        return nsa_attend(qb, gb, kc, vc, sel_blocks, band[:, :, 0], band[:, :, 1], qpos, bpos, nsa_tab)

    o_nsa = lax.map(nsa_step, (blocks(q), blocks(gates), jnp.swapaxes(bands, 0, 1), band_pos, qpos_blocks))
    o_nsa = jnp.swapaxes(o_nsa, 0, 1).reshape(b, s, NSA_W)
    k_pos = jnp.arange(s, dtype=jnp.int32)

    def diff_step(args):
        qb, qpos = args
        return diff_attend(qb, diff_kv, qpos, k_pos, diff_tab, lam, lam_init, subln_l)

    o_diff = lax.map(diff_step, (blocks(dq), qpos_blocks))
    o_diff = jnp.swapaxes(o_diff, 0, 1).reshape(b, s, DIFF_W)
    out = jnp.concatenate([o_nsa, o_diff], axis=-1) @ w_o_l
    w_keep = min(WINDOW, s)
    return out, (cmp_kv, slc_kv, diff_kv, win_kv[:, s - w_keep:])


def sample_mixer(h, past_cmp, past_slc, past_diff, win_buf, w_in_l, w_o_l, w_cmp1_l, w_cmp2_l, cmp_pe_l,
                 lam, lam_init, subln_l, nsa_tab, diff_tab):
    b, t, _ = h.shape
    past_len = past_cmp.shape[1]
    total = past_len + t
    pos = past_len + jnp.arange(t, dtype=jnp.int32)
    q, gates, cmp_kv, slc_kv, win_kv, dq, diff_kv = project(h, w_in_l)
    cmp_all = jnp.concatenate([past_cmp, cmp_kv], axis=1)
    nbc = total // CMP_BLOCK
    kc, vc = compress(cmp_all[:, :nbc * CMP_BLOCK], w_cmp1_l, w_cmp2_l, cmp_pe_l)
    sel_blocks = to_sel_blocks(jnp.concatenate([past_slc, slc_kv], axis=1))
    w_buf = win_buf.shape[1]
    win_all = jnp.concatenate([win_buf, win_kv], axis=1)
    win_pos = jnp.concatenate([past_len - w_buf + jnp.arange(w_buf, dtype=jnp.int32), pos])
    o_nsa = nsa_attend(q, gates, kc, vc, sel_blocks, win_all[:, :, 0], win_all[:, :, 1], pos, win_pos, nsa_tab)
    diff_all = jnp.concatenate([past_diff, diff_kv], axis=1)
    o_diff = diff_attend(dq, diff_all, pos, jnp.arange(total, dtype=jnp.int32), diff_tab, lam, lam_init, subln_l)
    out = jnp.concatenate([o_nsa, o_diff], axis=-1) @ w_o_l
    return out, (cmp_kv, slc_kv, diff_kv, win_all[:, t:])


def gather_past(cache, layer, page_table):
    g = cache[layer, page_table]
    return g.reshape((g.shape[0], g.shape[1] * g.shape[2]) + g.shape[3:])


def setup_inputs(seed: int = 0) -> dict:
    key = jax.random.key(seed)
    ks = jax.random.split(key, 32)
    n_pages = PAST_LEN // PAGE_SIZE
    n_used = DEC_BATCH * n_pages
    n_pool = n_used + max(1, n_used // 4)
    w_buf = min(WINDOW, PAST_LEN)

    def nrm(k, shape, s):
        return jax.random.normal(k, shape, jnp.float32) * s

    page_table = jax.random.permutation(ks[6], n_pool)[:n_used].reshape(DEC_BATCH, n_pages).astype(jnp.int32)
    return {
        'x_prompt': nrm(ks[0], (BATCH, SEQ, D_MODEL), 1.0),
        'x_sample': nrm(ks[1], (DEC_BATCH, DEC_SEQ, D_MODEL), 1.0),
        'cache_cmp_kv': nrm(ks[2], (DEPTH, n_pool, PAGE_SIZE, 2, NSA_KV_HEADS, NSA_HD), 1.0),
        'cache_slc_kv': nrm(ks[3], (DEPTH, n_pool, PAGE_SIZE, 2, NSA_KV_HEADS, NSA_HD), 1.0),
        'cache_diff_kv': nrm(ks[4], (DEPTH, n_pool, PAGE_SIZE, 2, DIFF_HEADS, DIFF_VD), 1.0),
        'state_win_kv': nrm(ks[5], (DEPTH, DEC_BATCH, w_buf, 2, NSA_KV_HEADS, NSA_HD), 1.0),
        'page_table': page_table,
        'w_in': nrm(ks[7], (DEPTH, D_MODEL, D_IN), D_MODEL ** -0.5),
        'w_o': nrm(ks[8], (DEPTH, NSA_W + DIFF_W, D_MODEL), (NSA_W + DIFF_W) ** -0.5),
        'w_cmp1': nrm(ks[9], (DEPTH, 2, CMP_BLOCK * NSA_HD, CMP_HIDDEN), (CMP_BLOCK * NSA_HD) ** -0.5),
        'w_cmp2': nrm(ks[10], (DEPTH, 2, CMP_HIDDEN, NSA_HD), CMP_HIDDEN ** -0.5),
        'cmp_pe': nrm(ks[11], (DEPTH, 2, CMP_BLOCK, NSA_HD), 0.1),
        'lam_q1': nrm(ks[12], (DEPTH, DIFF_QK), 0.1),
        'lam_k1': nrm(ks[13], (DEPTH, DIFF_QK), 0.1),
        'lam_q2': nrm(ks[14], (DEPTH, DIFF_QK), 0.1),
        'lam_k2': nrm(ks[15], (DEPTH, DIFF_QK), 0.1),
        'diff_subln': 1.0 + nrm(ks[16], (DEPTH, DIFF_VD), 0.02),
        'rel_bias': nrm(ks[17], (N_BUCKETS, N_ALL_HEADS), 0.5),
        'g_attn_pre': 1.0 + nrm(ks[18], (DEPTH, D_MODEL), 0.02),
        'g_attn_post': 1.0 + nrm(ks[19], (DEPTH, D_MODEL), 0.02),
        'g_mlp_pre': 1.0 + nrm(ks[20], (DEPTH, D_MODEL), 0.02),
        'g_mlp_post': 1.0 + nrm(ks[21], (DEPTH, D_MODEL), 0.02),
        'w_up': nrm(ks[22], (DEPTH, D_MODEL, D_FF), D_MODEL ** -0.5),
        'w_down': nrm(ks[23], (DEPTH, D_FF, D_MODEL), D_FF ** -0.5),
    }


def reference(x_prompt, x_sample, cache_cmp_kv, cache_slc_kv, cache_diff_kv, state_win_kv, page_table,
              w_in, w_o, w_cmp1, w_cmp2, cmp_pe, lam_q1, lam_k1, lam_q2, lam_k2, diff_subln, rel_bias,
              g_attn_pre, g_attn_post, g_mlp_pre, g_mlp_post, w_up, w_down):
    nsa_tab = rel_bias[:, :NSA_HEADS].reshape(N_BUCKETS, NSA_KV_HEADS, NSA_GROUP)
    diff_tab = rel_bias[:, NSA_HEADS:]
    xp, xs = x_prompt, x_sample
    p_states = ([], [], [], [])
    s_states = ([], [], [], [])
    for l in range(DEPTH):
        lam_init = 0.8 - 0.6 * math.exp(-0.3 * l)
        lam = (jnp.exp(jnp.sum((lam_q1[l] * lam_k1[l]).astype(jnp.float32)))
               - jnp.exp(jnp.sum((lam_q2[l] * lam_k2[l]).astype(jnp.float32))) + lam_init)
        mix_w = (w_in[l], w_o[l], w_cmp1[l], w_cmp2[l], cmp_pe[l], lam, lam_init, diff_subln[l], nsa_tab, diff_tab)
        o, st = prompt_mixer(rmsnorm(xp, g_attn_pre[l]), *mix_w)
        xp = xp + rmsnorm(o, g_attn_post[l])
        xp = xp + rmsnorm(sq_relu_mlp(rmsnorm(xp, g_mlp_pre[l]), w_up[l], w_down[l]), g_mlp_post[l])
        for lst, a in zip(p_states, st):
            lst.append(a)
        o, st = sample_mixer(rmsnorm(xs, g_attn_pre[l]),
                             gather_past(cache_cmp_kv, l, page_table),
                             gather_past(cache_slc_kv, l, page_table),
                             gather_past(cache_diff_kv, l, page_table),
                             state_win_kv[l], *mix_w)
        xs = xs + rmsnorm(o, g_attn_post[l])
        xs = xs + rmsnorm(sq_relu_mlp(rmsnorm(xs, g_mlp_pre[l]), w_up[l], w_down[l]), g_mlp_post[l])
        for lst, a in zip(s_states, st):
            lst.append(a)
    new_cmp_p = jnp.stack(p_states[0])
    new_slc_p = jnp.stack(p_states[1])
    new_diff_p = jnp.stack(p_states[2])
    new_win_p = jnp.stack(p_states[3])
    new_cmp_s = jnp.stack(s_states[0])
    new_slc_s = jnp.stack(s_states[1])
    new_diff_s = jnp.stack(s_states[2])
    new_win_s = jnp.stack(s_states[3])
    return (xp, xs, new_cmp_p, new_slc_p, new_diff_p, new_win_p, new_cmp_s, new_slc_s, new_diff_s, new_win_s)
```

```python
import functools
import math

import jax
import jax.numpy as jnp
from jax import lax
from jax.experimental import pallas as pl
from jax.experimental.pallas import tpu as pltpu

NSA_HEADS = 8
NSA_KV_HEADS = 2
NSA_GROUP = NSA_HEADS // NSA_KV_HEADS
NSA_HD = 64
CMP_BLOCK = 64
N_SELECT = 16
WINDOW = 512
DIFF_HEADS = 4
DIFF_QK = 64
DIFF_VD = 2 * DIFF_QK
N_BUCKETS = 32
MAX_DISTANCE = 128
PAGE_SIZE = 128
EPS = 1e-6
NEG = -1e30

NSA_W = NSA_HEADS * NSA_HD
DIFF_W = DIFF_HEADS * DIFF_VD
KV_W = NSA_KV_HEADS * NSA_HD
N_GATES = NSA_HEADS * 3
QK_SCALE = NSA_HD ** -0.5

LANES = 128
TQ = 128
TK = 128
ROW_TILE = 512
VMEM_LIMIT_BYTES = 56 * 1024 * 1024

F32 = jnp.float32
BF16 = jnp.bfloat16


def _cparams(*sem):
    return pltpu.CompilerParams(dimension_semantics=sem, vmem_limit_bytes=VMEM_LIMIT_BYTES)


def _dot(a, b):
    return jnp.dot(a, b, preferred_element_type=F32)


def _dot_nt(a, b):
    return lax.dot_general(a, b, (((1,), (1,)), ((), ())), preferred_element_type=F32)


def _rms_mm_kernel(x_ref, g_ref, w_ref, *o_refs, splits, act):
    x = x_ref[...]
    ms = jnp.mean(x * x, axis=-1, keepdims=True)
    h = ((x * lax.rsqrt(ms + EPS)) * g_ref[...]).astype(BF16)
    off = 0
    for o_ref, n in zip(o_refs, splits):
        y = _dot(h, w_ref[:, off:off + n])
        if act:
            y = jnp.square(jnp.maximum(y, 0.0))
        o_ref[...] = y.astype(o_ref.dtype)
        off += n


def rms_mm(x, g, w, splits, out_dtype, act=False, name="rms_mm"):
    m, k = x.shape
    tm = min(ROW_TILE, m)
    n = sum(splits)
    return pl.pallas_call(
        functools.partial(_rms_mm_kernel, splits=tuple(splits), act=act),
        grid=(m // tm,),
        in_specs=[pl.BlockSpec((tm, k), lambda i: (i, 0)),
                  pl.BlockSpec((1, k), lambda i: (0, 0)),
                  pl.BlockSpec((k, n), lambda i: (0, 0))],
        out_specs=[pl.BlockSpec((tm, s), lambda i: (i, 0)) for s in splits],
        out_shape=[jax.ShapeDtypeStruct((m, s), out_dtype) for s in splits],
        compiler_params=_cparams("parallel"),
        name=name,
    )(x, g, w)


def _mm_rms_res_kernel(a_ref, w_ref, g_ref, r_ref, o_ref):
    y = _dot(a_ref[...], w_ref[...])
    ms = jnp.mean(y * y, axis=-1, keepdims=True)
    o_ref[...] = r_ref[...] + (y * lax.rsqrt(ms + EPS)) * g_ref[...]


def mm_rms_res(a, w, g, res, name="mm_rms_res"):
    m, k = a.shape
    n = w.shape[1]
    tm = min(ROW_TILE, m)
    return pl.pallas_call(
        _mm_rms_res_kernel,
        grid=(m // tm,),
        in_specs=[pl.BlockSpec((tm, k), lambda i: (i, 0)),
                  pl.BlockSpec((k, n), lambda i: (0, 0)),
                  pl.BlockSpec((1, n), lambda i: (0, 0)),
                  pl.BlockSpec((tm, n), lambda i: (i, 0))],
        out_specs=pl.BlockSpec((tm, n), lambda i: (i, 0)),
        out_shape=jax.ShapeDtypeStruct((m, n), F32),
        compiler_params=_cparams("parallel"),
        name=name,
    )(a, w, g, res)


def _compress_body(x_refs, pe_ref, w1_ref, w2_ref, o_ref, acc_ref, nb):
    lane = lax.broadcasted_iota(jnp.int32, (nb, 2 * LANES), 1)
    head0 = (lane % LANES) < NSA_HD
    acc_ref[...] = jnp.zeros_like(acc_ref)

    def step(tp, carry):
        t0 = 2 * tp
        for c in range(2):
            a0 = x_refs[c][pl.ds(t0, nb, stride=CMP_BLOCK), :] + pe_ref[c, pl.ds(t0, 1), :]
            a1 = x_refs[c][pl.ds(t0 + 1, nb, stride=CMP_BLOCK), :] + pe_ref[c, pl.ds(t0 + 1, 1), :]
            ll = jnp.concatenate([a0, a1], axis=1)
            l2 = jnp.concatenate([jnp.where(head0, ll, 0.0), jnp.where(head0, 0.0, ll)], axis=0).astype(BF16)
            acc_ref[c] += _dot(l2, w1_ref[c, tp])
        return carry

    lax.fori_loop(0, CMP_BLOCK // 2, step, 0)
    for c in range(2):
        hid = jax.nn.gelu(acc_ref[c]).astype(BF16)
        o_ref[c] = _dot(hid, w2_ref[c])


def _compress_rows_kernel(xk_ref, xv_ref, pe_ref, w1_ref, w2_ref, o_ref, acc_ref, *, nb):
    _compress_body((xk_ref, xv_ref), pe_ref, w1_ref, w2_ref, o_ref, acc_ref, nb)


def _compress_pages_kernel(pt_ref, *refs, npg, nb):
    del pt_ref
    page_refs = refs[:npg]
    pe_ref, w1_ref, w2_ref, o_ref, x_ref, acc_ref = refs[npg:]
    for j in range(npg):
        for c in range(2):
            x_ref[c, j * PAGE_SIZE:(j + 1) * PAGE_SIZE, :] = page_refs[j][:, c * LANES:(c + 1) * LANES]
    _compress_body((x_ref.at[0], x_ref.at[1]), pe_ref, w1_ref, w2_ref, o_ref, acc_ref, nb)


def compress_rows(x, pe_rows, w1p, w2, rows_per_step):
    r = x.shape[0]
    steps = r // rows_per_step
    nb = rows_per_step // CMP_BLOCK
    return pl.pallas_call(
        functools.partial(_compress_rows_kernel, nb=nb),
        grid=(steps,),
        in_specs=[pl.BlockSpec((rows_per_step, LANES), lambda i: (i, 0)),
                  pl.BlockSpec((rows_per_step, LANES), lambda i: (i, 1)),
                  pl.BlockSpec(pe_rows.shape, lambda i: (0, 0, 0)),
                  pl.BlockSpec(w1p.shape, lambda i: (0, 0, 0, 0)),
                  pl.BlockSpec(w2.shape, lambda i: (0, 0, 0))],
        out_specs=pl.BlockSpec((None, 2, 2 * nb, NSA_HD), lambda i: (i, 0, 0, 0)),
        out_shape=jax.ShapeDtypeStruct((steps, 2, 2 * nb, NSA_HD), F32),
        scratch_shapes=[pltpu.VMEM((2, 2 * nb, 2 * LANES), F32)],
        compiler_params=_cparams("parallel"),
        name="compress_rows",
    )(x, x, pe_rows, w1p, w2)


def compress_pages(pages, page_ids, pe_rows, w1p, w2, npg):
    steps = page_ids.shape[0] // npg
    nb = npg * PAGE_SIZE // CMP_BLOCK

    def page_spec(j):
        return pl.BlockSpec((None, PAGE_SIZE, 2 * LANES), lambda i, pt: (pt[i * npg + j], 0, 0))

    grid_spec = pltpu.PrefetchScalarGridSpec(
        num_scalar_prefetch=1,
        grid=(steps,),
        in_specs=[page_spec(j) for j in range(npg)] + [
            pl.BlockSpec(pe_rows.shape, lambda i, pt: (0, 0, 0)),
            pl.BlockSpec(w1p.shape, lambda i, pt: (0, 0, 0, 0)),
            pl.BlockSpec(w2.shape, lambda i, pt: (0, 0, 0))],
        out_specs=pl.BlockSpec((None, 2, 2 * nb, NSA_HD), lambda i, pt: (i, 0, 0, 0)),
        scratch_shapes=[pltpu.VMEM((2, npg * PAGE_SIZE, LANES), F32),
                        pltpu.VMEM((2, 2 * nb, 2 * LANES), F32)],
    )
    return pl.pallas_call(
        functools.partial(_compress_pages_kernel, npg=npg, nb=nb),
        grid_spec=grid_spec,
        out_shape=jax.ShapeDtypeStruct((steps, 2, 2 * nb, NSA_HD), F32),
        compiler_params=_cparams("arbitrary"),
        name="compress_pages",
    )(page_ids, *([pages] * npg), pe_rows, w1p, w2)


def _tile_update(carry, k_tile, vt_tile, wq, bias, mask):
    m, l, acc = carry
    s = _dot(k_tile, wq) + bias
    if mask is not None:
        s = jnp.where(mask, s, NEG)
    m_new = jnp.maximum(m, jnp.max(s, axis=0, keepdims=True))
    alpha = jnp.exp(m - m_new)
    p = jnp.exp(s - m_new)
    l = alpha * l + jnp.sum(p, axis=0, keepdims=True)
    acc = alpha * acc + _dot(vt_tile, p.astype(BF16))
    return m_new, l, acc


def _init_carry(dv, nl):
    return (jnp.full((1, nl), NEG, F32), jnp.zeros((1, nl), F32), jnp.zeros((dv, nl), F32))


def _nsa_prompt_kernel(wq_ref, kc_ref, vct_ref, ks_ref, vst_ref, kw_ref, vwt_ref, gt_ref,
                       bt_ref, cfar_ref, bcn_ref, o_ref, sel_ref, *, nb, n_top):
    qi = pl.program_id(1)
    nl = NSA_GROUP * TQ
    krow = lax.broadcasted_iota(jnp.int32, (TK, nl), 0)
    rq = lax.broadcasted_iota(jnp.int32, (TK, nl), 1) % TQ
    qiv = jnp.broadcast_to(qi, (TK, nl))
    brow = lax.broadcasted_iota(jnp.int32, (nb, nl), 0)
    rqb = lax.broadcasted_iota(jnp.int32, (nb, nl), 1) % TQ
    cur = 2 * qi + (rqb >= CMP_BLOCK).astype(jnp.int32)
    jr = lax.broadcasted_iota(jnp.int32, (nb, TQ), 0).astype(F32)
    cur_q = (2 * qi + (lax.broadcasted_iota(jnp.int32, (nb, TQ), 1) >= CMP_BLOCK).astype(jnp.int32)).astype(F32)
    causal = krow <= rq
    win_old = rq < krow
    for kvh in range(NSA_KV_HEADS):
        hs = slice(kvh * NSA_HD, (kvh + 1) * NSA_HD)
        wq = (wq_ref[kvh].astype(F32) * QK_SCALE).astype(BF16)
        cfar = cfar_ref[kvh]
        jrel = brow - (2 * qi - 2)
        bias_c = jnp.broadcast_to(cfar, (nb, nl))
        for k in range(3):
            bias_c = jnp.where(jrel == k, bcn_ref[kvh, k:k + 1, :], bias_c)
        mask_c = brow < cur
        sc = jnp.where(mask_c, _dot(kc_ref[...], wq) + bias_c, NEG)
        mc = jnp.max(sc, axis=0, keepdims=True)
        pc = jnp.where(mask_c, jnp.exp(sc - mc), 0.0)
        den = jnp.sum(pc, axis=0, keepdims=True)
        pc = pc / jnp.where(den > 0, den, 1.0)
        o_c = _dot(vct_ref[hs, :], pc.astype(BF16))
        imp = pc[:, 0:TQ]
        for g in range(1, NSA_GROUP):
            imp = imp + pc[:, g * TQ:(g + 1) * TQ]
        imp = jnp.where(jr < cur_q, imp, -1.0)
        v = imp
        sel = jr == cur_q
        for _ in range(n_top):
            mx = jnp.max(v, axis=0, keepdims=True)
            idx = jnp.min(jnp.where(v == mx, jr, float(nb)), axis=0, keepdims=True)
            hit = jr == idx
            sel = sel | (hit & (mx >= 0.0))
            v = jnp.where(hit, -2.0, v)
        self = sel.astype(F32)
        sel_ref[kvh] = jnp.concatenate([self] * NSA_GROUP, axis=1)
        def sel_mask(kt):
            lo = sel_ref[kvh, pl.ds(2 * kt, 1), :]
            hi = sel_ref[kvh, pl.ds(2 * kt + 1, 1), :]
            return jnp.where(krow < CMP_BLOCK, lo, hi) > 0.5

        def far_body(kt, carry):
            return _tile_update(carry, ks_ref[kt], vst_ref[kt, hs, :], wq, cfar, sel_mask(kt))

        carry = lax.fori_loop(0, jnp.maximum(qi - 1, 0), far_body, _init_carry(NSA_HD, nl))
        kt1 = jnp.maximum(qi - 1, 0)
        carry = _tile_update(carry, ks_ref[kt1], vst_ref[kt1, hs, :], wq, bt_ref[kvh, 1],
                             sel_mask(kt1) & (qiv >= 1))
        m_s, l_s, a_s = _tile_update(carry, ks_ref[qi], vst_ref[qi, hs, :], wq, bt_ref[kvh, 0],
                                     sel_mask(qi) & causal)
        o_s = a_s / l_s
        carry = _init_carry(NSA_HD, nl)
        n_back = WINDOW // TK
        for o in range(n_back, -1, -1):
            kt = jnp.maximum(qi - o, 0)
            valid = qiv >= o
            if o == 0:
                bias, mask = bt_ref[kvh, 0], causal
            elif o == 1:
                bias, mask = bt_ref[kvh, 1], valid
            elif o == n_back:
                bias, mask = cfar, win_old & valid
            else:
                bias, mask = cfar, valid
            carry = _tile_update(carry, kw_ref[kt], vwt_ref[kt, hs, :], wq, bias, mask)
        m_w, l_w, a_w = carry
        o_w = a_w / l_w
        for g in range(NSA_GROUP):
            h = kvh * NSA_GROUP + g
            gs = jax.nn.sigmoid(gt_ref[3 * h:3 * h + 3, :])
            ls = slice(g * TQ, (g + 1) * TQ)
            o = gs[0:1] * o_c[:, ls] + gs[1:2] * o_s[:, ls] + gs[2:3] * o_w[:, ls]
            o_ref[h * NSA_HD:(h + 1) * NSA_HD, :] = o.astype(o_ref.dtype)


def nsa_prompt(wq, kc, vct, ks, vst, kw, vwt, gt, bt, cfar, bcn):
    b, _, nqt, _, nl = wq.shape
    nb = kc.shape[1]
    n_top = min(N_SELECT - 1, nb)
    t = nqt * TQ
    full = lambda shape: pl.BlockSpec((None,) + shape, lambda bi, qi: (bi,) + (0,) * len(shape))
    const = lambda arr: pl.BlockSpec(arr.shape, lambda bi, qi: (0,) * arr.ndim)
    return pl.pallas_call(
        functools.partial(_nsa_prompt_kernel, nb=nb, n_top=n_top),
        grid=(b, nqt),
        in_specs=[pl.BlockSpec((None, NSA_KV_HEADS, None, LANES, nl), lambda bi, qi: (bi, 0, qi, 0, 0)),
                  full((nb, LANES)), full((LANES, nb)),
                  full((nqt, TK, LANES)), full((nqt, LANES, TK)),
                  full((nqt, TK, LANES)), full((nqt, LANES, TK)),
                  pl.BlockSpec((None, 32, TQ), lambda bi, qi: (bi, 0, qi)),
                  const(bt), const(cfar), const(bcn)],
        out_specs=pl.BlockSpec((None, NSA_W, TQ), lambda bi, qi: (bi, 0, qi)),
        out_shape=jax.ShapeDtypeStruct((b, NSA_W, t), BF16),
        scratch_shapes=[pltpu.VMEM((NSA_KV_HEADS, nb, nl), F32)],
        compiler_params=_cparams("parallel", "arbitrary"),
        name="nsa_prompt",
    )(wq, kc, vct, ks, vst, kw, vwt, gt, bt, cfar, bcn)


def _lambda_value(lam_ref, cst_ref):
    lv = lam_ref[...]
    s1 = jnp.sum(lv[0:1] * lv[1:2], axis=1, keepdims=True)
    s2 = jnp.sum(lv[2:3] * lv[3:4], axis=1, keepdims=True)
    lam_init = cst_ref[0:1, 0:1]
    return jnp.exp(s1) - jnp.exp(s2) + lam_init, lam_init


def _diff_prompt_kernel(wq_ref, k_ref, vt_ref, bt_ref, cfar_ref, lam_ref, sub_ref, cst_ref, o_ref):
    qi = pl.program_id(2)
    nl = 2 * TQ
    krow = lax.broadcasted_iota(jnp.int32, (TK, nl), 0)
    rq = lax.broadcasted_iota(jnp.int32, (TK, nl), 1) % TQ
    causal = krow <= rq
    wq = (wq_ref[...].astype(F32) * QK_SCALE).astype(BF16)
    cfar = cfar_ref[...]

    def far_body(kt, carry):
        return _tile_update(carry, k_ref[kt], vt_ref[kt], wq, cfar, None)

    carry = lax.fori_loop(0, jnp.maximum(qi - 1, 0), far_body, _init_carry(DIFF_VD, nl))
    kt1 = jnp.maximum(qi - 1, 0)
    carry = _tile_update(carry, k_ref[kt1], vt_ref[kt1], wq, bt_ref[1],
                         jnp.broadcast_to(qi, (TK, nl)) >= 1)
    m, l, acc = _tile_update(carry, k_ref[qi], vt_ref[qi], wq, bt_ref[0], causal)
    a = acc / l
    lam, lam_init = _lambda_value(lam_ref, cst_ref)
    o = a[:, :TQ] - lam * a[:, TQ:]
    ms = jnp.mean(o * o, axis=0, keepdims=True)
    y = ((o * lax.rsqrt(ms + EPS)) * sub_ref[...]) * (1.0 - lam_init)
    o_ref[...] = y.astype(o_ref.dtype)


def diff_prompt(wq, k, vt, bt, cfar, lam_vecs, sub_col, cst):
    b, nh, nqt, _, nl = wq.shape
    t = nqt * TQ
    return pl.pallas_call(
        _diff_prompt_kernel,
        grid=(b, nh, nqt),
        in_specs=[pl.BlockSpec((None, None, None, LANES, nl), lambda bi, h, qi: (bi, h, qi, 0, 0)),
                  pl.BlockSpec((None, None, nqt, TK, LANES), lambda bi, h, qi: (bi, h, 0, 0, 0)),
                  pl.BlockSpec((None, None, nqt, DIFF_VD, TK), lambda bi, h, qi: (bi, h, 0, 0, 0)),
                  pl.BlockSpec((None, 2, TK, nl), lambda bi, h, qi: (h, 0, 0, 0)),
                  pl.BlockSpec((None, 1, nl), lambda bi, h, qi: (h, 0, 0)),
                  pl.BlockSpec(lam_vecs.shape, lambda bi, h, qi: (0, 0)),
                  pl.BlockSpec(sub_col.shape, lambda bi, h, qi: (0, 0)),
                  pl.BlockSpec(cst.shape, lambda bi, h, qi: (0, 0))],
        out_specs=pl.BlockSpec((None, DIFF_VD, TQ), lambda bi, h, qi: (bi, h, qi)),
        out_shape=jax.ShapeDtypeStruct((b, DIFF_W, t), BF16),
        compiler_params=_cparams("parallel", "parallel", "arbitrary"),
        name="diff_prompt",
    )(wq, k, vt, bt, cfar, lam_vecs, sub_col, cst)


def _round_bf16(x):
    return x.astype(BF16).astype(F32)


def _sample_kernel(pt_ref, *refs, npg, n_top, nb_past):
    del pt_ref
    slc_refs = refs[:npg]
    diff_refs = refs[npg:2 * npg]
    (qs_ref, qd_ref, gate_ref, kcvc_ref, slc_new_ref, win_new_ref, dkv_new_ref, win_ref,
     bias_c_ref, bias_s_ref, bias_w_ref, bias_d_ref, bnew_ref, expand_ref,
     lam_ref, sub_ref, cst_ref, o_nsa_ref, o_diff_ref) = refs[2 * npg:]
    past = npg * PAGE_SIZE

    qs_f = qs_ref[...] * QK_SCALE
    qs = qs_f.astype(BF16)
    kcvc = kcvc_ref[...].astype(BF16)
    sc = _dot_nt(qs, kcvc) + bias_c_ref[...]
    mc = jnp.max(sc, axis=1, keepdims=True)
    ec = jnp.exp(sc - mc)
    pc = ec / jnp.sum(ec, axis=1, keepdims=True)
    o_c = _dot(pc.astype(BF16), kcvc)
    ri = lax.broadcasted_iota(jnp.int32, (LANES, LANES), 0)
    ci = lax.broadcasted_iota(jnp.int32, (LANES, LANES), 1)
    sel_rows = []
    for kvh in range(NSA_KV_HEADS):
        imp = jnp.sum(pc[kvh * NSA_GROUP:(kvh + 1) * NSA_GROUP], axis=0, keepdims=True)
        rmat = jnp.broadcast_to(imp, (LANES, LANES))
        cmat = rmat.T
        beats = ((cmat > rmat) | ((cmat == rmat) & (ri < ci))) & (ri < nb_past)
        rank = jnp.sum(beats.astype(F32), axis=0, keepdims=True)
        sel_rows.append(jnp.where((rank < n_top) & (ci[0:1] < nb_past), 1.0, 0.0))
    hrow = lax.broadcasted_iota(jnp.int32, (NSA_HEADS, LANES), 0)
    sel8 = jnp.where(hrow < NSA_GROUP, sel_rows[0], sel_rows[1]).astype(BF16)
    sel_mask = _dot(sel8, expand_ref[...]) > 0.5
    s_parts = [_dot_nt(qs, slc_refs[j][...].astype(BF16)) for j in range(npg)]
    s = jnp.where(sel_mask, jnp.concatenate(s_parts, axis=1) + bias_s_ref[...], NEG)
    new_s = _round_bf16(slc_new_ref[...])
    qs_r = qs.astype(F32)
    s_new = jnp.sum(qs_r * new_s, axis=1, keepdims=True) + bnew_ref[:, 0:1]
    m = jnp.maximum(jnp.max(s, axis=1, keepdims=True), s_new)
    pf = jnp.exp(s - m)
    p_new = jnp.exp(s_new - m)
    den = jnp.sum(pf, axis=1, keepdims=True) + p_new
    p = pf.astype(BF16)
    acc = _round_bf16(p_new) * new_s
    for j in range(npg):
        acc = acc + _dot(p[:, j * PAGE_SIZE:(j + 1) * PAGE_SIZE], slc_refs[j][...].astype(BF16))
    o_s = acc / den
    wk = win_ref[...].astype(BF16)
    wl = lax.broadcasted_iota(jnp.int32, (NSA_HEADS, wk.shape[0]), 1)
    s = jnp.where(wl >= 1, _dot_nt(qs, wk) + bias_w_ref[...], NEG)
    new_w = _round_bf16(win_new_ref[...])
    s_new = jnp.sum(qs_r * new_w, axis=1, keepdims=True) + bnew_ref[:, 0:1]
    m = jnp.maximum(jnp.max(s, axis=1, keepdims=True), s_new)
    pf = jnp.exp(s - m)
    p_new = jnp.exp(s_new - m)
    den = jnp.sum(pf, axis=1, keepdims=True) + p_new
    p = pf.astype(BF16)
    o_w = (_dot(p, wk) + _round_bf16(p_new) * new_w) / den
    gs = jax.nn.sigmoid(gate_ref[...])
    o_nsa_ref[...] = gs[:, 0:1] * o_c + gs[:, 1:2] * o_s + gs[:, 2:3] * o_w
    kw_d = DIFF_HEADS * 2 * DIFF_QK
    qd = (qd_ref[...] * QK_SCALE).astype(BF16)
    s_parts = [_dot_nt(qd, diff_refs[j][:, 0:kw_d].astype(BF16)) for j in range(npg)]
    s = jnp.concatenate(s_parts, axis=1) + bias_d_ref[...]
    new_d = _round_bf16(dkv_new_ref[...])
    s_new = jnp.sum(qd.astype(F32) * new_d[:, 0:kw_d], axis=1, keepdims=True) + bnew_ref[:, 1:2]
    m = jnp.maximum(jnp.max(s, axis=1, keepdims=True), s_new)
    pf = jnp.exp(s - m)
    p_new = jnp.exp(s_new - m)
    den = jnp.sum(pf, axis=1, keepdims=True) + p_new
    p = pf.astype(BF16)
    acc = _round_bf16(p_new) * new_d[:, kw_d:]
    for j in range(npg):
        acc = acc + _dot(p[:, j * PAGE_SIZE:(j + 1) * PAGE_SIZE], diff_refs[j][:, kw_d:].astype(BF16))
    a = acc / den
    lam, lam_init = _lambda_value(lam_ref, cst_ref)
    d = a[0:DIFF_HEADS] - lam * a[DIFF_HEADS:]
    own = (lax.broadcasted_iota(jnp.int32, d.shape, 1) // DIFF_VD) == lax.broadcasted_iota(jnp.int32, d.shape, 0)
    d = jnp.where(own, d, 0.0)
    ms = jnp.sum(d * d, axis=1, keepdims=True) * (1.0 / DIFF_VD)
    y = ((d * lax.rsqrt(ms + EPS)) * sub_ref[...]) * (1.0 - lam_init)
    o_diff_ref[...] = jnp.sum(y, axis=0, keepdims=True)


def sample_mixer(page_ids, slc_pages, diff_pages, qs, qd, gate8, kcvc, slc_new, win_new, dkv_new, win_state,
                 bias_c, bias_s, bias_w, bias_d, bnew, expand, lam_vecs, sub_row, cst, npg):
    nbatch = qs.shape[0]
    nb_past = npg * PAGE_SIZE // CMP_BLOCK
    n_top = min(N_SELECT - 1, nb_past)

    def page_spec(width, j):
        return pl.BlockSpec((None, PAGE_SIZE, width), lambda i, pt: (pt[i * npg + j], 0, 0))

    def per_b(arr):
        return pl.BlockSpec((None,) + arr.shape[1:], lambda i, pt: (i,) + (0,) * (arr.ndim - 1))

    def const(arr):
        return pl.BlockSpec(arr.shape, lambda i, pt: (0,) * arr.ndim)

    per_b_in = [qs, qd, gate8, kcvc, slc_new, win_new, dkv_new, win_state]
    const_in = [bias_c, bias_s, bias_w, bias_d, bnew, expand, lam_vecs, sub_row, cst]
    grid_spec = pltpu.PrefetchScalarGridSpec(
        num_scalar_prefetch=1,
        grid=(nbatch,),
        in_specs=([page_spec(slc_pages.shape[-1], j) for j in range(npg)]
                  + [page_spec(diff_pages.shape[-1], j) for j in range(npg)]
                  + [per_b(a) for a in per_b_in] + [const(a) for a in const_in]),
        out_specs=[pl.BlockSpec((None, NSA_HEADS, 2 * LANES), lambda i, pt: (i, 0, 0)),
                   pl.BlockSpec((None, 1, DIFF_W), lambda i, pt: (i, 0, 0))],
    )
    return pl.pallas_call(
        functools.partial(_sample_kernel, npg=npg, n_top=n_top, nb_past=nb_past),
        grid_spec=grid_spec,
        out_shape=[jax.ShapeDtypeStruct((nbatch, NSA_HEADS, 2 * LANES), F32),
                   jax.ShapeDtypeStruct((nbatch, 1, DIFF_W), F32)],
        compiler_params=_cparams("arbitrary"),
        name="sample_mixer",
    )(page_ids, *([slc_pages] * npg), *([diff_pages] * npg), *per_b_in, *const_in)


def _t5_bucket(dist):
    n = jnp.maximum(dist, 0)
    max_exact = N_BUCKETS // 2
    nf = jnp.maximum(n, 1).astype(F32)
    large = max_exact + (jnp.log(nf / max_exact) / math.log(MAX_DISTANCE / max_exact)
                         * (N_BUCKETS - max_exact)).astype(jnp.int32)
    large = jnp.minimum(large, N_BUCKETS - 1)
    return jnp.where(n < max_exact, n, large)


def _prompt_bias_tables(rel_bias):
    c = jnp.arange(TK, dtype=jnp.int32)[:, None]
    r = jnp.arange(TQ, dtype=jnp.int32)[None, :]
    near = jnp.stack([rel_bias[_t5_bucket(TQ * o + r - c)] for o in range(2)])
    near = jnp.transpose(near, (3, 0, 1, 2))
    far = rel_bias[N_BUCKETS - 1]
    k3 = jnp.arange(3, dtype=jnp.int32)[:, None]
    cmp_near = jnp.transpose(rel_bias[_t5_bucket(r + (CMP_BLOCK + 1) - CMP_BLOCK * k3)], (2, 0, 1))
    nn = near[:NSA_HEADS].reshape(NSA_KV_HEADS, NSA_GROUP, 2, TK, TQ)
    bt_n = jnp.transpose(nn, (0, 2, 3, 1, 4)).reshape(NSA_KV_HEADS, 2, TK, NSA_GROUP * TQ)
    cfar_n = jnp.repeat(far[:NSA_HEADS].reshape(NSA_KV_HEADS, 1, NSA_GROUP), TQ, axis=-1)
    cn = cmp_near[:NSA_HEADS].reshape(NSA_KV_HEADS, NSA_GROUP, 3, TQ)
    bcn = jnp.transpose(cn, (0, 2, 1, 3)).reshape(NSA_KV_HEADS, 3, NSA_GROUP * TQ)
    bcn = jnp.pad(bcn, ((0, 0), (0, 5), (0, 0)))
    nd = near[NSA_HEADS:]
    bt_d = jnp.concatenate([nd, nd], axis=-1)
    cfar_d = jnp.repeat(far[NSA_HEADS:].reshape(DIFF_HEADS, 1, 1), 2 * TQ, axis=-1)
    return bt_n, cfar_n, bcn, bt_d, cfar_d


def _sample_bias_tables(rel_bias, past_len, w_buf):
    nsa_tab = rel_bias[:, :NSA_HEADS]
    diff_tab = rel_bias[:, NSA_HEADS:]
    diff_tab8 = jnp.concatenate([diff_tab, diff_tab], axis=1)
    pos = past_len
    k = jnp.arange(past_len, dtype=jnp.int32)
    bias_s = nsa_tab[_t5_bucket(pos - k)].T
    bias_d = diff_tab8[_t5_bucket(pos - k)].T
    wpos = past_len - w_buf + jnp.arange(w_buf, dtype=jnp.int32)
    bias_w = nsa_tab[_t5_bucket(pos - wpos)].T
    nb = past_len // CMP_BLOCK
    blk = jnp.arange(nb, dtype=jnp.int32)
    bias_c = nsa_tab[_t5_bucket(pos - (blk * CMP_BLOCK + CMP_BLOCK - 1))].T
    bias_c = jnp.pad(bias_c, ((0, 0), (0, LANES - nb)), constant_values=NEG)
    bnew = jnp.stack([nsa_tab[0], diff_tab8[0]], axis=1)
    bnew = jnp.pad(bnew, ((0, 0), (0, LANES - 2)))
    expand = (jnp.arange(LANES, dtype=jnp.int32)[:, None] == (k[None, :] // CMP_BLOCK)).astype(BF16)
    return bias_c, bias_s, bias_w, bias_d, bnew, expand


def kernel(x_prompt, x_sample, cache_cmp_kv, cache_slc_kv, cache_diff_kv, state_win_kv, page_table,
           w_in, w_o, w_cmp1, w_cmp2, cmp_pe, lam_q1, lam_k1, lam_q2, lam_k2, diff_subln, rel_bias,
           g_attn_pre, g_attn_post, g_mlp_pre, g_mlp_post, w_up, w_down):
    depth = w_in.shape[0]
    b, t, dm = x_prompt.shape
    nbatch = x_sample.shape[0]
    n_pool = cache_cmp_kv.shape[1]
    npg = page_table.shape[1]
    past_len = npg * PAGE_SIZE
    w_buf = state_win_kv.shape[2]
    nqt = t // TQ
    nb = t // CMP_BLOCK
    assert t % TQ == 0 and x_sample.shape[1] == 1 and w_buf == WINDOW and past_len % CMP_BLOCK == 0

    g_lo, g_hi = NSA_W, NSA_W + N_GATES
    proj_splits = (NSA_W, 2 * KV_W, 2 * KV_W, 2 * KV_W, DIFF_HEADS * 2 * DIFF_QK, 2 * DIFF_W, LANES)

    bt_n, cfar_n, bcn, bt_d, cfar_d = _prompt_bias_tables(rel_bias)
    bias_c, bias_s, bias_w, bias_d, bnew, expand = _sample_bias_tables(rel_bias, past_len, w_buf)

    cmp_pages = cache_cmp_kv.reshape(depth * n_pool, PAGE_SIZE, 2 * KV_W)
    slc_pages = cache_slc_kv.reshape(depth * n_pool, PAGE_SIZE, 2 * KV_W)
    diff_pages = cache_diff_kv.reshape(depth * n_pool, PAGE_SIZE, 2 * DIFF_W)

    xp = x_prompt.reshape(b * t, dm)
    xs = x_sample.reshape(nbatch, dm)
    p_states = ([], [], [], [])
    s_states = ([], [], [], [])

    for l in range(depth):
        lam_init = 0.8 - 0.6 * math.exp(-0.3 * l)
        w_l = w_in[l]
        w_proj = jnp.concatenate(
            [w_l[:, :g_lo], w_l[:, g_hi:], w_l[:, g_lo:g_hi], jnp.zeros((dm, LANES - N_GATES), F32)],
            axis=1).astype(BF16)
        w_o_l = w_o[l].astype(BF16)
        w_up_l = w_up[l].astype(BF16)
        w_down_l = w_down[l].astype(BF16)
        w1 = w_cmp1[l].reshape(2, CMP_BLOCK // 2, 2, NSA_HD, -1)
        w1p = jnp.concatenate([w1[:, :, 0], w1[:, :, 0], w1[:, :, 1], w1[:, :, 1]], axis=2).astype(BF16)
        w2 = w_cmp2[l].astype(BF16)
        pe = cmp_pe[l]
        pe_rows = jnp.concatenate([pe, pe], axis=2)
        lam_vecs = jnp.stack([lam_q1[l], lam_k1[l], lam_q2[l], lam_k2[l]])
        cst = jnp.full((1, LANES), lam_init, F32)
        sub = diff_subln[l]

        q, cmp, slc, win, dq, dkv, gates = rms_mm(xp, g_attn_pre[l][None], w_proj, proj_splits, F32,
                                                  name="proj_prompt")
        comp = compress_rows(cmp, pe_rows, w1p, w2, rows_per_step=t)
        comp = comp.reshape(b, 2, NSA_KV_HEADS, nb, NSA_HD)
        kc = jnp.transpose(comp[:, 0], (0, 2, 1, 3)).reshape(b, nb, KV_W).astype(BF16)
        vct = jnp.transpose(comp[:, 1], (0, 1, 3, 2)).reshape(b, KV_W, nb).astype(BF16)

        def key_tiles(a):
            return a.astype(BF16).reshape(b, nqt, TK, a.shape[-1])

        def val_tiles_t(a):
            return jnp.transpose(a.astype(BF16).reshape(b, nqt, TK, a.shape[-1]), (0, 1, 3, 2))

        q5 = q.astype(BF16).reshape(b, nqt, TQ, NSA_KV_HEADS, NSA_GROUP, NSA_HD)
        wq_half = jnp.transpose(q5, (0, 3, 1, 5, 4, 2)).reshape(b, NSA_KV_HEADS, nqt, NSA_HD, NSA_GROUP * TQ)
        zeros_half = jnp.zeros_like(wq_half[:, 0])
        wq_n = jnp.stack([jnp.concatenate([wq_half[:, 0], zeros_half], axis=2),
                          jnp.concatenate([zeros_half, wq_half[:, 1]], axis=2)], axis=1)
        gt = jnp.transpose(gates.reshape(b, t, LANES)[:, :, :32], (0, 2, 1))
        o_nsa_t = nsa_prompt(wq_n, kc, vct,
                             key_tiles(slc[:, :KV_W]), val_tiles_t(slc[:, KV_W:]),
                             key_tiles(win[:, :KV_W]), val_tiles_t(win[:, KV_W:]),
                             gt, bt_n, cfar_n, bcn)

        dq5 = dq.astype(BF16).reshape(b, nqt, TQ, DIFF_HEADS, 2, DIFF_QK)
        dq_t = jnp.transpose(dq5, (0, 3, 1, 4, 5, 2))
        zq = jnp.zeros_like(dq_t[:, :, :, 0])
        wq_d = jnp.concatenate([jnp.concatenate([dq_t[:, :, :, 0], zq], axis=-1),
                                jnp.concatenate([zq, dq_t[:, :, :, 1]], axis=-1)], axis=-2)
        kd = dkv[:, :DIFF_W].astype(BF16).reshape(b, nqt, TK, DIFF_HEADS, 2 * DIFF_QK)
        kd = jnp.transpose(kd, (0, 3, 1, 2, 4))
        vd = dkv[:, DIFF_W:].astype(BF16).reshape(b, nqt, TK, DIFF_HEADS, DIFF_VD)
        vd_t = jnp.transpose(vd, (0, 3, 1, 4, 2))
        o_diff_t = diff_prompt(wq_d, kd, vd_t, bt_d, cfar_d, lam_vecs, sub[:, None], cst)

        o_cat = jnp.transpose(jnp.concatenate([o_nsa_t, o_diff_t], axis=1), (0, 2, 1)).reshape(b * t, dm)
        xp = mm_rms_res(o_cat, w_o_l, g_attn_post[l][None], xp, name="oproj_prompt")
        (hid,) = rms_mm(xp, g_mlp_pre[l][None], w_up_l, (w_up_l.shape[1],), BF16, act=True, name="mlp_up_prompt")
        xp = mm_rms_res(hid, w_down_l, g_mlp_post[l][None], xp, name="mlp_down_prompt")

        p_states[0].append(cmp.reshape(b, t, 2, NSA_KV_HEADS, NSA_HD))
        p_states[1].append(slc.reshape(b, t, 2, NSA_KV_HEADS, NSA_HD))
        p_states[2].append(dkv.reshape(b, t, 2, DIFF_HEADS, DIFF_VD))
        w_keep = min(WINDOW, t)
        p_states[3].append(win.reshape(b, t, 2, NSA_KV_HEADS, NSA_HD)[:, t - w_keep:])

        q, cmp, slc, win, dq, dkv, gates = rms_mm(xs, g_attn_pre[l][None], w_proj, proj_splits, F32,
                                                  name="proj_sample")
        page_ids = (page_table + l * n_pool).reshape(-1).astype(jnp.int32)
        npg_step = 4 * npg
        comp = compress_pages(cmp_pages, page_ids, pe_rows, w1p, w2, npg_step)
        nb_past = past_len // CMP_BLOCK
        comp = comp.reshape(nbatch // 4, 2, NSA_KV_HEADS, 4, nb_past, NSA_HD)
        kcvc = jnp.transpose(comp, (0, 3, 4, 1, 2, 5)).reshape(nbatch, nb_past, 2 * KV_W)
        kcvc = jnp.pad(kcvc, ((0, 0), (0, LANES - nb_past), (0, 0)))

        q4 = q.reshape(nbatch, NSA_KV_HEADS, NSA_GROUP, NSA_HD)
        zq4 = jnp.zeros_like(q4[:, 0])
        qs = jnp.concatenate([jnp.concatenate([q4[:, 0], zq4, zq4, zq4], axis=-1),
                              jnp.concatenate([zq4, q4[:, 1], zq4, zq4], axis=-1)], axis=1)
        dq4 = jnp.transpose(dq.reshape(nbatch, DIFF_HEADS, 2, DIFF_QK), (0, 2, 1, 3))
        eye_h = jnp.eye(DIFF_HEADS, dtype=F32)[None, :, :, None, None]
        eye_m = jnp.eye(2, dtype=F32)[:, None, None, :, None]
        qd = (dq4[:, :, :, None, None, :] * (eye_h * eye_m)).reshape(nbatch, 2 * DIFF_HEADS, DIFF_W)
        gate8 = jnp.pad(gates[:, :N_GATES].reshape(nbatch, NSA_HEADS, 3), ((0, 0), (0, 0), (0, LANES - 3)))
        sub_row = jnp.tile(sub, DIFF_HEADS)[None]
        o_nsa8, o_diff = sample_mixer(
            page_ids, slc_pages, diff_pages, qs, qd, gate8, kcvc,
            slc[:, None], win[:, None], dkv[:, None], state_win_kv[l].reshape(nbatch, w_buf, 2 * KV_W),
            bias_c, bias_s, bias_w, bias_d, bnew, expand, lam_vecs, sub_row, cst, npg)
        o8 = o_nsa8.reshape(nbatch, NSA_KV_HEADS, NSA_GROUP, 4, NSA_HD)
        o_nsa = jnp.stack([o8[:, 0, :, 2], o8[:, 1, :, 3]], axis=1).reshape(nbatch, NSA_W)
        o_cat = jnp.concatenate([o_nsa, o_diff.reshape(nbatch, DIFF_W)], axis=1).astype(BF16)
        xs = mm_rms_res(o_cat, w_o_l, g_attn_post[l][None], xs, name="oproj_sample")
        (hid,) = rms_mm(xs, g_mlp_pre[l][None], w_up_l, (w_up_l.shape[1],), BF16, act=True, name="mlp_up_sample")
        xs = mm_rms_res(hid, w_down_l, g_mlp_post[l][None], xs, name="mlp_down_sample")

        s_states[0].append(cmp.reshape(nbatch, 1, 2, NSA_KV_HEADS, NSA_HD))
        s_states[1].append(slc.reshape(nbatch, 1, 2, NSA_KV_HEADS, NSA_HD))
        s_states[2].append(dkv.reshape(nbatch, 1, 2, DIFF_HEADS, DIFF_VD))
        win_new = win.reshape(nbatch, 1, 2, NSA_KV_HEADS, NSA_HD)
        s_states[3].append(jnp.concatenate([state_win_kv[l][:, 1:], win_new], axis=1))

    return (xp.reshape(b, t, dm), xs.reshape(nbatch, 1, dm),
            jnp.stack(p_states[0]), jnp.stack(p_states[1]), jnp.stack(p_states[2]), jnp.stack(p_states[3]),
            jnp.stack(s_states[0]), jnp.stack(s_states[1]), jnp.stack(s_states[2]), jnp.stack(s_states[3]))
```

```python
import functools
import math

import jax
import jax.numpy as jnp
from jax import lax
from jax.experimental import pallas as pl
from jax.experimental.pallas import tpu as pltpu

NSA_HEADS = 8
NSA_KV_HEADS = 2
NSA_GROUP = NSA_HEADS // NSA_KV_HEADS
NSA_HD = 64
CMP_BLOCK = 64
N_SELECT = 16
WINDOW = 512
DIFF_HEADS = 4
DIFF_QK = 64
DIFF_VD = 2 * DIFF_QK
N_BUCKETS = 32
MAX_DISTANCE = 128
PAGE_SIZE = 128
EPS = 1e-6
NEG = -1e30

NSA_W = NSA_HEADS * NSA_HD
DIFF_W = DIFF_HEADS * DIFF_VD
KV_W = NSA_KV_HEADS * NSA_HD
N_GATES = NSA_HEADS * 3
QK_SCALE = NSA_HD ** -0.5

LANES = 128
TQ = 128
TK = 128
ROW_TILE = 512
FAR_TILES = 8
WIN_TILES = WINDOW // TK + 1
BLK_PER_TILE = TK // CMP_BLOCK
DIFF_HPS = 2
VMEM_LIMIT_BYTES = 56 * 1024 * 1024

F32 = jnp.float32
BF16 = jnp.bfloat16


def _cparams(*sem):
    return pltpu.CompilerParams(dimension_semantics=sem, vmem_limit_bytes=VMEM_LIMIT_BYTES)


def _dot(a, b):
    return jnp.dot(a, b, preferred_element_type=F32)


def _dot_nt(a, b):
    return lax.dot_general(a, b, (((1,), (1,)), ((), ())), preferred_element_type=F32)


def _rms_mm_kernel(x_ref, g_ref, w_ref, *o_refs, splits, act):
    x = x_ref[...]
    ms = jnp.mean(x * x, axis=-1, keepdims=True)
    h = ((x * lax.rsqrt(ms + EPS)) * g_ref[...]).astype(BF16)
    off = 0
    for o_ref, n in zip(o_refs, splits):
        y = _dot(h, w_ref[:, off:off + n])
        if act:
            y = jnp.square(jnp.maximum(y, 0.0))
        o_ref[...] = y.astype(o_ref.dtype)
        off += n


def rms_mm(x, g, w, splits, out_dtype, act=False, name="rms_mm"):
    m, k = x.shape
    tm = min(ROW_TILE, m)
    n = sum(splits)
    return pl.pallas_call(
        functools.partial(_rms_mm_kernel, splits=tuple(splits), act=act),
        grid=(m // tm,),
        in_specs=[pl.BlockSpec((tm, k), lambda i: (i, 0)),
                  pl.BlockSpec((1, k), lambda i: (0, 0)),
                  pl.BlockSpec((k, n), lambda i: (0, 0))],
        out_specs=[pl.BlockSpec((tm, s), lambda i: (i, 0)) for s in splits],
        out_shape=[jax.ShapeDtypeStruct((m, s), out_dtype) for s in splits],
        compiler_params=_cparams("parallel"),
        name=name,
    )(x, g, w)


def _mm_rms_res_kernel(a_ref, w_ref, g_ref, r_ref, o_ref):
    y = _dot(a_ref[...], w_ref[...])
    ms = jnp.mean(y * y, axis=-1, keepdims=True)
    o_ref[...] = r_ref[...] + (y * lax.rsqrt(ms + EPS)) * g_ref[...]


def mm_rms_res(a, w, g, res, name="mm_rms_res"):
    m, k = a.shape
    n = w.shape[1]
    tm = min(ROW_TILE, m)
    return pl.pallas_call(
        _mm_rms_res_kernel,
        grid=(m // tm,),
        in_specs=[pl.BlockSpec((tm, k), lambda i: (i, 0)),
                  pl.BlockSpec((k, n), lambda i: (0, 0)),
                  pl.BlockSpec((1, n), lambda i: (0, 0)),
                  pl.BlockSpec((tm, n), lambda i: (i, 0))],
        out_specs=pl.BlockSpec((tm, n), lambda i: (i, 0)),
        out_shape=jax.ShapeDtypeStruct((m, n), F32),
        compiler_params=_cparams("parallel"),
        name=name,
    )(a, w, g, res)


def _compress_body(x_refs, pe_ref, w1_ref, w2_ref, o_ref, acc_ref, nb):
    lane = lax.broadcasted_iota(jnp.int32, (nb, 2 * LANES), 1)
    head0 = (lane % LANES) < NSA_HD
    acc_ref[...] = jnp.zeros_like(acc_ref)

    def step(tp, carry):
        t0 = 2 * tp
        for c in range(2):
            a0 = x_refs[c][pl.ds(t0, nb, stride=CMP_BLOCK), :] + pe_ref[c, pl.ds(t0, 1), :]
            a1 = x_refs[c][pl.ds(t0 + 1, nb, stride=CMP_BLOCK), :] + pe_ref[c, pl.ds(t0 + 1, 1), :]
            ll = jnp.concatenate([a0, a1], axis=1)
            l2 = jnp.concatenate([jnp.where(head0, ll, 0.0), jnp.where(head0, 0.0, ll)], axis=0).astype(BF16)
            acc_ref[c] += _dot(l2, w1_ref[c, tp])
        return carry

    lax.fori_loop(0, CMP_BLOCK // 2, step, 0)
    for c in range(2):
        hid = jax.nn.gelu(acc_ref[c]).astype(BF16)
        o_ref[c] = _dot(hid, w2_ref[c])


def _compress_rows_kernel(xk_ref, xv_ref, pe_ref, w1_ref, w2_ref, o_ref, acc_ref, *, nb):
    _compress_body((xk_ref, xv_ref), pe_ref, w1_ref, w2_ref, o_ref, acc_ref, nb)


def _compress_pages_kernel(pt_ref, *refs, npg):
    del pt_ref
    page_refs = refs[:npg]
    pe_ref, w1_ref, w2_ref, o_ref, x_ref, acc_ref = refs[npg:]
    n_ch = 2 * NSA_KV_HEADS
    for j in range(npg):
        for ch in range(n_ch):
            x_ref[ch, j * NSA_HD:(j + 1) * NSA_HD, :] = page_refs[j][ch * NSA_HD:(ch + 1) * NSA_HD, :]
    lane = lax.broadcasted_iota(jnp.int32, (npg, 2 * LANES), 1)
    even = (lane % LANES) < CMP_BLOCK
    acc_ref[...] = jnp.zeros_like(acc_ref)

    def step(dp, carry):
        d0 = 2 * dp
        for c in range(2):
            parts = []
            for h in range(NSA_KV_HEADS):
                xr = x_ref.at[c * NSA_KV_HEADS + h]
                a0 = xr[pl.ds(d0, npg, stride=NSA_HD), :] + pe_ref[c, pl.ds(d0, 1), :]
                a1 = xr[pl.ds(d0 + 1, npg, stride=NSA_HD), :] + pe_ref[c, pl.ds(d0 + 1, 1), :]
                ll = jnp.concatenate([a0, a1], axis=1)
                parts += [jnp.where(even, ll, 0.0), jnp.where(even, 0.0, ll)]
            acc_ref[c] += _dot(jnp.concatenate(parts, axis=0).astype(BF16), w1_ref[c, dp])
        return carry

    lax.fori_loop(0, NSA_HD // 2, step, 0)
    for c in range(2):
        hid = jax.nn.gelu(acc_ref[c]).astype(BF16)
        o_ref[c] = _dot(hid, w2_ref[c])


def compress_rows(x, pe_rows, w1p, w2, rows_per_step):
    r = x.shape[0]
    steps = r // rows_per_step
    nb = rows_per_step // CMP_BLOCK
    return pl.pallas_call(
        functools.partial(_compress_rows_kernel, nb=nb),
        grid=(steps,),
        in_specs=[pl.BlockSpec((rows_per_step, LANES), lambda i: (i, 0)),
                  pl.BlockSpec((rows_per_step, LANES), lambda i: (i, 1)),
                  pl.BlockSpec(pe_rows.shape, lambda i: (0, 0, 0)),
                  pl.BlockSpec(w1p.shape, lambda i: (0, 0, 0, 0)),
                  pl.BlockSpec(w2.shape, lambda i: (0, 0, 0))],
        out_specs=pl.BlockSpec((None, 2, 2 * nb, NSA_HD), lambda i: (i, 0, 0, 0)),
        out_shape=jax.ShapeDtypeStruct((steps, 2, 2 * nb, NSA_HD), F32),
        scratch_shapes=[pltpu.VMEM((2, 2 * nb, 2 * LANES), F32)],
        compiler_params=_cparams("parallel"),
        name="compress_rows",
    )(x, x, pe_rows, w1p, w2)


def compress_pages(pages, page_ids, pe_t, w1d, w2, npg):
    steps = page_ids.shape[0] // npg
    n_rows = 2 * NSA_KV_HEADS * npg

    def page_spec(j):
        return pl.BlockSpec((None,) + pages.shape[1:], lambda i, pt: (pt[i * npg + j], 0, 0))

    grid_spec = pltpu.PrefetchScalarGridSpec(
        num_scalar_prefetch=1,
        grid=(steps,),
        in_specs=[page_spec(j) for j in range(npg)] + [
            pl.BlockSpec(pe_t.shape, lambda i, pt: (0, 0, 0)),
            pl.BlockSpec(w1d.shape, lambda i, pt: (0, 0, 0, 0)),
            pl.BlockSpec(w2.shape, lambda i, pt: (0, 0, 0))],
        out_specs=pl.BlockSpec((None, 2, n_rows, NSA_HD), lambda i, pt: (i, 0, 0, 0)),
        scratch_shapes=[pltpu.VMEM((2 * NSA_KV_HEADS, npg * NSA_HD, LANES), F32),
                        pltpu.VMEM((2, n_rows, 2 * LANES), F32)],
    )
    return pl.pallas_call(
        functools.partial(_compress_pages_kernel, npg=npg),
        grid_spec=grid_spec,
        out_shape=jax.ShapeDtypeStruct((steps, 2, n_rows, NSA_HD), F32),
        compiler_params=_cparams("arbitrary"),
        name="compress_pages",
    )(page_ids, *([pages] * npg), pe_t, w1d, w2)


def _softmax_update(carry, s, vt_parts):
    m, l, acc = carry
    m_new = jnp.maximum(m, jnp.max(s, axis=0, keepdims=True))
    alpha = jnp.exp(m - m_new)
    p = jnp.exp(s - m_new)
    l = alpha * l + jnp.sum(p, axis=0, keepdims=True)
    pb = p.astype(BF16)
    pv = _dot(vt_parts[0], pb[0:TK])
    for j in range(1, len(vt_parts)):
        pv = pv + _dot(vt_parts[j], pb[j * TK:(j + 1) * TK])
    return m_new, l, alpha * acc + pv


def _init_carry(dv, nl):
    return (jnp.full((1, nl), NEG, F32), jnp.zeros((1, nl), F32), jnp.zeros((dv, nl), F32))


def _add_rows(s, rows, rows_per):
    n = rows.shape[0]
    return jnp.concatenate([s[i * rows_per:(i + 1) * rows_per] + rows[i:i + 1] for i in range(n)], axis=0)


def _valid_rows(conds, nl, reps=1):
    rows = []
    for cond in conds:
        rows.extend([jnp.full((1, nl), jnp.where(cond, 0.0, NEG), F32)] * reps)
    return jnp.concatenate(rows, axis=0)


def _nsa_prompt_kernel(wq_ref, kc_ref, vct_ref, ks_ref, vst_ref, kw_ref, vwt_ref, gt_ref,
                       near_ref, winb_ref, bcn_ref, o_ref, sel_ref, *, nb, n_top):
    qi = pl.program_id(1)
    nl = NSA_GROUP * TQ
    brow = lax.broadcasted_iota(jnp.int32, (nb, nl), 0)
    rqb = lax.broadcasted_iota(jnp.int32, (nb, nl), 1) % TQ
    cur = BLK_PER_TILE * qi + (rqb >= CMP_BLOCK).astype(jnp.int32)
    jr = lax.broadcasted_iota(jnp.int32, (nb, TQ), 0).astype(F32)
    cur_q = (BLK_PER_TILE * qi
             + (lax.broadcasted_iota(jnp.int32, (nb, TQ), 1) >= CMP_BLOCK).astype(jnp.int32)).astype(F32)
    hs = [slice(kvh * NSA_HD, (kvh + 1) * NSA_HD) for kvh in range(NSA_KV_HEADS)]
    wqs, o_cs = [], []
    for kvh in range(NSA_KV_HEADS):
        wq = (wq_ref[kvh].astype(F32) * QK_SCALE).astype(BF16)
        wqs.append(wq)
        jrel = brow - (BLK_PER_TILE * qi - 2)
        bias_c = jnp.zeros((nb, nl), F32)
        for k in range(3):
            bias_c = jnp.where(jrel == k, bcn_ref[kvh, k:k + 1, :], bias_c)
        mask_c = brow < cur
        sc = jnp.where(mask_c, _dot(kc_ref[...], wq) + bias_c, NEG)
        mc = jnp.max(sc, axis=0, keepdims=True)
        pc = jnp.where(mask_c, jnp.exp(sc - mc), 0.0)
        den = jnp.sum(pc, axis=0, keepdims=True)
        pc = pc / jnp.where(den > 0, den, 1.0)
        o_cs.append(_dot(vct_ref[hs[kvh], :], pc.astype(BF16)))
        imp = pc[:, 0:TQ]
        for g in range(1, NSA_GROUP):
            imp = imp + pc[:, g * TQ:(g + 1) * TQ]
        imp = jnp.where(jr < cur_q, imp, -1.0)
        v = imp
        sel = jr == cur_q
        for _ in range(n_top):
            mx = jnp.max(v, axis=0, keepdims=True)
            idx = jnp.min(jnp.where(v == mx, jr, float(nb)), axis=0, keepdims=True)
            hit = jr == idx
            sel = sel | (hit & (mx >= 0.0))
            v = jnp.where(hit, -2.0, v)
        sel_add = jnp.where(sel, 0.0, NEG)
        sel_ref[kvh] = jnp.concatenate([sel_add] * NSA_GROUP, axis=1)

    def far_chunk(c, carries, limit):
        t0 = c * FAR_TILES
        k_rows = ks_ref[pl.ds(t0, FAR_TILES)].reshape(FAR_TILES * TK, LANES)
        out = []
        for kvh in range(NSA_KV_HEADS):
            rows = sel_ref[kvh, pl.ds(pl.multiple_of(t0 * BLK_PER_TILE, 8), FAR_TILES * BLK_PER_TILE), :]
            if limit is not None:
                rows = rows + _valid_rows([t0 + j <= limit for j in range(FAR_TILES)], nl, BLK_PER_TILE)
            s = _add_rows(_dot(k_rows, wqs[kvh]), rows, CMP_BLOCK)
            vts = [vst_ref[t0 + j, hs[kvh], :] for j in range(FAR_TILES)]
            out.append(_softmax_update(carries[kvh], s, vts))
        return tuple(out)

    n_full = jnp.maximum(qi - 1, 0) // FAR_TILES
    carries = tuple(_init_carry(NSA_HD, nl) for _ in range(NSA_KV_HEADS))
    carries = lax.fori_loop(0, n_full, lambda c, cr: far_chunk(c, cr, None), carries)
    carries = far_chunk(n_full, carries, qi - 2)
    kt1 = jnp.maximum(qi - 1, 0)
    k_near = jnp.concatenate([ks_ref[kt1], ks_ref[qi]], axis=0)
    k_win = kw_ref[pl.ds(qi, WIN_TILES)].reshape(WIN_TILES * TK, LANES)
    prev_ok = _valid_rows([qi >= 1], nl)
    win_ok = _valid_rows([qi >= o for o in range(WIN_TILES - 1, -1, -1)], nl)
    for kvh in range(NSA_KV_HEADS):
        rows = jnp.concatenate(
            [sel_ref[kvh, pl.ds(BLK_PER_TILE * kt1 + i, 1), :] + prev_ok for i in range(BLK_PER_TILE)]
            + [sel_ref[kvh, pl.ds(BLK_PER_TILE * qi + i, 1), :] for i in range(BLK_PER_TILE)], axis=0)
        s = _add_rows(_dot(k_near, wqs[kvh]) + near_ref[kvh], rows, CMP_BLOCK)
        _, l_s, a_s = _softmax_update(carries[kvh], s, [vst_ref[kt1, hs[kvh], :], vst_ref[qi, hs[kvh], :]])
        o_s = a_s / l_s
        s = _add_rows(_dot(k_win, wqs[kvh]) + winb_ref[kvh], win_ok, TK)
        _, l_w, a_w = _softmax_update(_init_carry(NSA_HD, nl), s,
                                      [vwt_ref[qi + j, hs[kvh], :] for j in range(WIN_TILES)])
        o_w = a_w / l_w
        for g in range(NSA_GROUP):
            h = kvh * NSA_GROUP + g
            gs = jax.nn.sigmoid(gt_ref[3 * h:3 * h + 3, :])
            ls = slice(g * TQ, (g + 1) * TQ)
            o = gs[0:1] * o_cs[kvh][:, ls] + gs[1:2] * o_s[:, ls] + gs[2:3] * o_w[:, ls]
            o_ref[h * NSA_HD:(h + 1) * NSA_HD, :] = o.astype(o_ref.dtype)


def nsa_prompt(wq, kc, vct, ks, vst, kw, vwt, gt, near_b, win_b, bcn):
    b, _, nqt, _, nl = wq.shape
    nb = kc.shape[1]
    n_top = min(N_SELECT - 1, nb)
    t = nqt * TQ
    assert nqt % FAR_TILES == 0
    full = lambda shape: pl.BlockSpec((None,) + shape, lambda bi, qi: (bi,) + (0,) * len(shape))
    const = lambda arr: pl.BlockSpec(arr.shape, lambda bi, qi: (0,) * arr.ndim)
    return pl.pallas_call(
        functools.partial(_nsa_prompt_kernel, nb=nb, n_top=n_top),
        grid=(b, nqt),
        in_specs=[pl.BlockSpec((None, NSA_KV_HEADS, None, LANES, nl), lambda bi, qi: (bi, 0, qi, 0, 0)),
                  full((nb, LANES)), full((LANES, nb)),
                  full((nqt, TK, LANES)), full((nqt, LANES, TK)),
                  full((nqt + WIN_TILES - 1, TK, LANES)), full((nqt + WIN_TILES - 1, LANES, TK)),
                  pl.BlockSpec((None, 32, TQ), lambda bi, qi: (bi, 0, qi)),
                  const(near_b), const(win_b), const(bcn)],
        out_specs=pl.BlockSpec((None, NSA_W, TQ), lambda bi, qi: (bi, 0, qi)),
        out_shape=jax.ShapeDtypeStruct((b, NSA_W, t), BF16),
        scratch_shapes=[pltpu.VMEM((NSA_KV_HEADS, nb, nl), F32)],
        compiler_params=_cparams("parallel", "arbitrary"),
        name="nsa_prompt",
    )(wq, kc, vct, ks, vst, kw, vwt, gt, near_b, win_b, bcn)


def _lambda_value(lam_ref, cst_ref):
    lv = lam_ref[...]
    s1 = jnp.sum(lv[0:1] * lv[1:2], axis=1, keepdims=True)
    s2 = jnp.sum(lv[2:3] * lv[3:4], axis=1, keepdims=True)
    lam_init = cst_ref[0:1, 0:1]
    return jnp.exp(s1) - jnp.exp(s2) + lam_init, lam_init


def _diff_prompt_kernel(wq_ref, k_ref, vt_ref, near_ref, lam_ref, sub_ref, cst_ref, o_ref):
    qi = pl.program_id(2)
    nl = 2 * TQ
    wqs = [(wq_ref[h].astype(F32) * QK_SCALE).astype(BF16) for h in range(DIFF_HPS)]

    def far_chunk(c, carries, limit):
        t0 = c * FAR_TILES
        out = []
        for h in range(DIFF_HPS):
            s = _dot(k_ref[h, pl.ds(t0, FAR_TILES)].reshape(FAR_TILES * TK, LANES), wqs[h])
            if limit is not None:
                s = _add_rows(s, _valid_rows([t0 + j <= limit for j in range(FAR_TILES)], nl), TK)
            out.append(_softmax_update(carries[h], s, [vt_ref[h, t0 + j] for j in range(FAR_TILES)]))
        return tuple(out)

    n_full = jnp.maximum(qi - 1, 0) // FAR_TILES
    carries = tuple(_init_carry(DIFF_VD, nl) for _ in range(DIFF_HPS))
    carries = lax.fori_loop(0, n_full, lambda c, cr: far_chunk(c, cr, None), carries)
    carries = far_chunk(n_full, carries, qi - 2)
    kt1 = jnp.maximum(qi - 1, 0)
    lam, lam_init = _lambda_value(lam_ref, cst_ref)
    near_ok = _valid_rows([qi >= 1, True], nl)
    for h in range(DIFF_HPS):
        k_near = jnp.concatenate([k_ref[h, kt1], k_ref[h, qi]], axis=0)
        s = _add_rows(_dot(k_near, wqs[h]) + near_ref[h], near_ok, TK)
        _, l, acc = _softmax_update(carries[h], s, [vt_ref[h, kt1], vt_ref[h, qi]])
        a = acc / l
        o = a[:, :TQ] - lam * a[:, TQ:]
        ms = jnp.mean(o * o, axis=0, keepdims=True)
        y = ((o * lax.rsqrt(ms + EPS)) * sub_ref[...]) * (1.0 - lam_init)
        o_ref[h * DIFF_VD:(h + 1) * DIFF_VD, :] = y.astype(o_ref.dtype)


def diff_prompt(wq, k, vt, near_b, lam_vecs, sub_col, cst):
    b, nh, nqt, _, nl = wq.shape
    t = nqt * TQ
    assert nqt % FAR_TILES == 0 and nh % DIFF_HPS == 0
    return pl.pallas_call(
        _diff_prompt_kernel,
        grid=(b, nh // DIFF_HPS, nqt),
        in_specs=[pl.BlockSpec((None, DIFF_HPS, None, LANES, nl), lambda bi, h, qi: (bi, h, qi, 0, 0)),
                  pl.BlockSpec((None, DIFF_HPS, nqt, TK, LANES), lambda bi, h, qi: (bi, h, 0, 0, 0)),
                  pl.BlockSpec((None, DIFF_HPS, nqt, DIFF_VD, TK), lambda bi, h, qi: (bi, h, 0, 0, 0)),
                  pl.BlockSpec((DIFF_HPS, 2 * TK, nl), lambda bi, h, qi: (h, 0, 0)),
                  pl.BlockSpec(lam_vecs.shape, lambda bi, h, qi: (0, 0)),
                  pl.BlockSpec(sub_col.shape, lambda bi, h, qi: (0, 0)),
                  pl.BlockSpec(cst.shape, lambda bi, h, qi: (0, 0))],
        out_specs=pl.BlockSpec((None, DIFF_HPS * DIFF_VD, TQ), lambda bi, h, qi: (bi, h, qi)),
        out_shape=jax.ShapeDtypeStruct((b, DIFF_W, t), BF16),
        compiler_params=_cparams("parallel", "parallel", "arbitrary"),
        name="diff_prompt",
    )(wq, k, vt, near_b, lam_vecs, sub_col, cst)


def _round_bf16(x):
    return x.astype(BF16).astype(F32)


def _sample_kernel(pt_ref, *refs, npg, n_top, nb_past):
    del pt_ref
    slc_refs = refs[:npg]
    diff_refs = refs[npg:2 * npg]
    (qs_ref, qd_ref, gate_ref, kcvc_ref, slc_new_ref, win_new_ref, dkv_new_ref, win_ref,
     bias_c_ref, bias_s_ref, bias_w_ref, bias_d_ref, bnew_ref, expand_ref,
     lam_ref, sub_ref, cst_ref, o_nsa_ref, o_diff_ref) = refs[2 * npg:]
    past = npg * PAGE_SIZE

    qs_f = qs_ref[...] * QK_SCALE
    qs = qs_f.astype(BF16)
    kcvc = kcvc_ref[...].astype(BF16)
    sc = _dot(qs, kcvc) + bias_c_ref[...]
    mc = jnp.max(sc, axis=1, keepdims=True)
    ec = jnp.exp(sc - mc)
    pc = ec / jnp.sum(ec, axis=1, keepdims=True)
    o_c = _dot_nt(pc.astype(BF16), kcvc)
    ri = lax.broadcasted_iota(jnp.int32, (LANES, LANES), 0)
    ci = lax.broadcasted_iota(jnp.int32, (LANES, LANES), 1)
    sel_rows = []
    for kvh in range(NSA_KV_HEADS):
        imp = jnp.sum(pc[kvh * NSA_GROUP:(kvh + 1) * NSA_GROUP], axis=0, keepdims=True)
        rmat = jnp.broadcast_to(imp, (LANES, LANES))
        cmat = rmat.T
        beats = ((cmat > rmat) | ((cmat == rmat) & (ri < ci))) & (ri < nb_past)
        rank = jnp.sum(beats.astype(F32), axis=0, keepdims=True)
        sel_rows.append(jnp.where((rank < n_top) & (ci[0:1] < nb_past), 1.0, 0.0))
    hrow = lax.broadcasted_iota(jnp.int32, (NSA_HEADS, LANES), 0)
    sel8 = jnp.where(hrow < NSA_GROUP, sel_rows[0], sel_rows[1]).astype(BF16)
    sel_mask = _dot(sel8, expand_ref[...]) > 0.5
    s_parts = [_dot(qs, slc_refs[j][...].astype(BF16)) for j in range(npg)]
    s = jnp.where(sel_mask, jnp.concatenate(s_parts, axis=1) + bias_s_ref[...], NEG)
    new_s = _round_bf16(slc_new_ref[...])
    qs_r = qs.astype(F32)
    s_new = jnp.sum(qs_r * new_s, axis=1, keepdims=True) + bnew_ref[:, 0:1]
    m = jnp.maximum(jnp.max(s, axis=1, keepdims=True), s_new)
    pf = jnp.exp(s - m)
    p_new = jnp.exp(s_new - m)
    den = jnp.sum(pf, axis=1, keepdims=True) + p_new
    p = pf.astype(BF16)
    acc = _round_bf16(p_new) * new_s
    for j in range(npg):
        acc = acc + _dot_nt(p[:, j * PAGE_SIZE:(j + 1) * PAGE_SIZE], slc_refs[j][...].astype(BF16))
    o_s = acc / den
    wk = win_ref[...].astype(BF16)
    wl = lax.broadcasted_iota(jnp.int32, (NSA_HEADS, wk.shape[1]), 1)
    s = jnp.where(wl >= 1, _dot(qs, wk) + bias_w_ref[...], NEG)
    new_w = _round_bf16(win_new_ref[...])
    s_new = jnp.sum(qs_r * new_w, axis=1, keepdims=True) + bnew_ref[:, 0:1]
    m = jnp.maximum(jnp.max(s, axis=1, keepdims=True), s_new)
    pf = jnp.exp(s - m)
    p_new = jnp.exp(s_new - m)
    den = jnp.sum(pf, axis=1, keepdims=True) + p_new
    p = pf.astype(BF16)
    o_w = (_dot_nt(p, wk) + _round_bf16(p_new) * new_w) / den
    gs = jax.nn.sigmoid(gate_ref[...])
    o_nsa_ref[...] = gs[:, 0:1] * o_c + gs[:, 1:2] * o_s + gs[:, 2:3] * o_w
    kw_d = DIFF_HEADS * 2 * DIFF_QK
    qd = (qd_ref[...] * QK_SCALE).astype(BF16)
    def page_cat(j, first):
        parts = [diff_refs[j][pl.ds(first + h, PAGE_SIZE, stride=2 * DIFF_HEADS), :] for h in range(DIFF_HEADS)]
        return jnp.concatenate(parts, axis=1).astype(BF16)

    s_parts = [_dot_nt(qd, page_cat(j, 0)) for j in range(npg)]
    s = jnp.concatenate(s_parts, axis=1) + bias_d_ref[...]
    new_d = _round_bf16(dkv_new_ref[...])
    s_new = jnp.sum(qd.astype(F32) * new_d[:, 0:kw_d], axis=1, keepdims=True) + bnew_ref[:, 1:2]
    m = jnp.maximum(jnp.max(s, axis=1, keepdims=True), s_new)
    pf = jnp.exp(s - m)
    p_new = jnp.exp(s_new - m)
    den = jnp.sum(pf, axis=1, keepdims=True) + p_new
    p = pf.astype(BF16)
    acc = _round_bf16(p_new) * new_d[:, kw_d:]
    for j in range(npg):
        acc = acc + _dot(p[:, j * PAGE_SIZE:(j + 1) * PAGE_SIZE], page_cat(j, DIFF_HEADS))
    a = acc / den
    lam, lam_init = _lambda_value(lam_ref, cst_ref)
    d = a[0:DIFF_HEADS] - lam * a[DIFF_HEADS:]
    own = (lax.broadcasted_iota(jnp.int32, d.shape, 1) // DIFF_VD) == lax.broadcasted_iota(jnp.int32, d.shape, 0)
    d = jnp.where(own, d, 0.0)
    ms = jnp.sum(d * d, axis=1, keepdims=True) * (1.0 / DIFF_VD)
    y = ((d * lax.rsqrt(ms + EPS)) * sub_ref[...]) * (1.0 - lam_init)
    o_diff_ref[...] = jnp.sum(y, axis=0, keepdims=True)


def sample_mixer(page_ids, slc_pages, diff_pages, qs, qd, gate8, kcvc, slc_new, win_new, dkv_new, win_state,
                 bias_c, bias_s, bias_w, bias_d, bnew, expand, lam_vecs, sub_row, cst, npg):
    nbatch = qs.shape[0]
    nb_past = npg * PAGE_SIZE // CMP_BLOCK
    n_top = min(N_SELECT - 1, nb_past)

    def page_spec(pages, j):
        return pl.BlockSpec((None,) + pages.shape[1:], lambda i, pt: (pt[i * npg + j], 0, 0))

    def per_b(arr):
        return pl.BlockSpec((None,) + arr.shape[1:], lambda i, pt: (i,) + (0,) * (arr.ndim - 1))

    def const(arr):
        return pl.BlockSpec(arr.shape, lambda i, pt: (0,) * arr.ndim)

    per_b_in = [qs, qd, gate8, kcvc, slc_new, win_new, dkv_new, win_state]
    const_in = [bias_c, bias_s, bias_w, bias_d, bnew, expand, lam_vecs, sub_row, cst]
    grid_spec = pltpu.PrefetchScalarGridSpec(
        num_scalar_prefetch=1,
        grid=(nbatch,),
        in_specs=([page_spec(slc_pages, j) for j in range(npg)]
                  + [page_spec(diff_pages, j) for j in range(npg)]
                  + [per_b(a) for a in per_b_in] + [const(a) for a in const_in]),
        out_specs=[pl.BlockSpec((None, NSA_HEADS, 2 * LANES), lambda i, pt: (i, 0, 0)),
                   pl.BlockSpec((None, 1, DIFF_W), lambda i, pt: (i, 0, 0))],
    )
    return pl.pallas_call(
        functools.partial(_sample_kernel, npg=npg, n_top=n_top, nb_past=nb_past),
        grid_spec=grid_spec,
        out_shape=[jax.ShapeDtypeStruct((nbatch, NSA_HEADS, 2 * LANES), F32),
                   jax.ShapeDtypeStruct((nbatch, 1, DIFF_W), F32)],
        compiler_params=_cparams("arbitrary"),
        name="sample_mixer",
    )(page_ids, *([slc_pages] * npg), *([diff_pages] * npg), *per_b_in, *const_in)


def _t5_bucket(dist):
    n = jnp.maximum(dist, 0)
    max_exact = N_BUCKETS // 2
    nf = jnp.maximum(n, 1).astype(F32)
    large = max_exact + (jnp.log(nf / max_exact) / math.log(MAX_DISTANCE / max_exact)
                         * (N_BUCKETS - max_exact)).astype(jnp.int32)
    large = jnp.minimum(large, N_BUCKETS - 1)
    return jnp.where(n < max_exact, n, large)


def _prompt_bias_tables(rel_bias):
    c = jnp.arange(TK, dtype=jnp.int32)[:, None]
    r = jnp.arange(TQ, dtype=jnp.int32)[None, :]
    near = jnp.stack([rel_bias[_t5_bucket(TQ * o + r - c)] for o in range(2)])
    near = jnp.transpose(near, (3, 0, 1, 2))
    far = rel_bias[N_BUCKETS - 1]
    k3 = jnp.arange(3, dtype=jnp.int32)[:, None]
    cmp_near = jnp.transpose(rel_bias[_t5_bucket(r + (CMP_BLOCK + 1) - CMP_BLOCK * k3)], (2, 0, 1))
    near = near - far[:, None, None, None]
    cmp_near = cmp_near - far[:, None, None]
    causal_add = jnp.where(c <= r, 0.0, NEG)
    old_add = jnp.where(r < c, 0.0, NEG)

    def lanes(per_head):
        x = per_head.reshape((NSA_KV_HEADS, NSA_GROUP) + per_head.shape[1:])
        return jnp.transpose(x, (0, 2, 1, 3)).reshape(NSA_KV_HEADS, per_head.shape[1], NSA_GROUP * TQ)

    near_n = near[:NSA_HEADS]
    two = jnp.concatenate([near_n[:, 1], near_n[:, 0] + causal_add], axis=1)
    near_b = lanes(two)
    mid = jnp.zeros((NSA_HEADS, (WIN_TILES - 3) * TK, TQ), F32)
    win = jnp.concatenate([jnp.broadcast_to(old_add, (NSA_HEADS, TK, TQ)), mid, two], axis=1)
    win_b = lanes(win)
    bcn = jnp.pad(lanes(cmp_near[:NSA_HEADS]), ((0, 0), (0, 5), (0, 0)))
    nd = near[NSA_HEADS:]
    two_d = jnp.concatenate([nd[:, 1], nd[:, 0] + causal_add], axis=1)
    near_d = jnp.concatenate([two_d, two_d], axis=-1)
    return near_b, win_b, bcn, near_d


def _sample_bias_tables(rel_bias, past_len, w_buf):
    nsa_tab = rel_bias[:, :NSA_HEADS]
    diff_tab = rel_bias[:, NSA_HEADS:]
    diff_tab8 = jnp.concatenate([diff_tab, diff_tab], axis=1)
    pos = past_len
    k = jnp.arange(past_len, dtype=jnp.int32)
    bias_s = nsa_tab[_t5_bucket(pos - k)].T
    bias_d = diff_tab8[_t5_bucket(pos - k)].T
    wpos = past_len - w_buf + jnp.arange(w_buf, dtype=jnp.int32)
    bias_w = nsa_tab[_t5_bucket(pos - wpos)].T
    nb = past_len // CMP_BLOCK
    blk = jnp.arange(nb, dtype=jnp.int32)
    bias_c = nsa_tab[_t5_bucket(pos - (blk * CMP_BLOCK + CMP_BLOCK - 1))].T
    bias_c = jnp.pad(bias_c, ((0, 0), (0, LANES - nb)), constant_values=NEG)
    bnew = jnp.stack([nsa_tab[0], diff_tab8[0]], axis=1)
    bnew = jnp.pad(bnew, ((0, 0), (0, LANES - 2)))
    expand = (jnp.arange(LANES, dtype=jnp.int32)[:, None] == (k[None, :] // CMP_BLOCK)).astype(BF16)
    return bias_c, bias_s, bias_w, bias_d, bnew, expand


def kernel(x_prompt, x_sample, cache_cmp_kv, cache_slc_kv, cache_diff_kv, state_win_kv, page_table,
           w_in, w_o, w_cmp1, w_cmp2, cmp_pe, lam_q1, lam_k1, lam_q2, lam_k2, diff_subln, rel_bias,
           g_attn_pre, g_attn_post, g_mlp_pre, g_mlp_post, w_up, w_down):
    depth = w_in.shape[0]
    b, t, dm = x_prompt.shape
    nbatch = x_sample.shape[0]
    n_pool = cache_cmp_kv.shape[1]
    npg = page_table.shape[1]
    past_len = npg * PAGE_SIZE
    w_buf = state_win_kv.shape[2]
    nqt = t // TQ
    nb = t // CMP_BLOCK
    assert t % TQ == 0 and x_sample.shape[1] == 1 and w_buf == WINDOW and past_len % CMP_BLOCK == 0

    g_lo, g_hi = NSA_W, NSA_W + N_GATES
    proj_splits = (NSA_W, 2 * KV_W, 2 * KV_W, 2 * KV_W, DIFF_HEADS * 2 * DIFF_QK, 2 * DIFF_W, LANES)

    near_b, win_b, bcn, near_d = _prompt_bias_tables(rel_bias)
    bias_c, bias_s, bias_w, bias_d, bnew, expand = _sample_bias_tables(rel_bias, past_len, w_buf)

    def feature_major(c):
        return jnp.transpose(c, (0, 1, 3, 4, 5, 2)).reshape(depth * n_pool, 2 * KV_W, PAGE_SIZE)

    cmp_pages = feature_major(cache_cmp_kv)
    slc_pages = feature_major(cache_slc_kv)
    diff_pages = cache_diff_kv.reshape(depth * n_pool, PAGE_SIZE * 2 * DIFF_HEADS, DIFF_VD)
    win_state = jnp.transpose(state_win_kv, (0, 1, 3, 4, 5, 2)).reshape(depth, nbatch, 2 * KV_W, w_buf)

    xp = x_prompt.reshape(b * t, dm)
    xs = x_sample.reshape(nbatch, dm)
    p_states = ([], [], [], [])
    s_states = ([], [], [], [])

    for l in range(depth):
        lam_init = 0.8 - 0.6 * math.exp(-0.3 * l)
        w_l = w_in[l]
        w_proj = jnp.concatenate(
            [w_l[:, :g_lo], w_l[:, g_hi:], w_l[:, g_lo:g_hi], jnp.zeros((dm, LANES - N_GATES), F32)],
            axis=1).astype(BF16)
        w_o_l = w_o[l].astype(BF16)
        w_up_l = w_up[l].astype(BF16)
        w_down_l = w_down[l].astype(BF16)
        w1 = w_cmp1[l].reshape(2, CMP_BLOCK // 2, 2, NSA_HD, -1)
        w1p = jnp.concatenate([w1[:, :, 0], w1[:, :, 0], w1[:, :, 1], w1[:, :, 1]], axis=2).astype(BF16)
        w2 = w_cmp2[l].astype(BF16)
        pe = cmp_pe[l]
        pe_rows = jnp.concatenate([pe, pe], axis=2)
        lam_vecs = jnp.stack([lam_q1[l], lam_k1[l], lam_q2[l], lam_k2[l]])
        cst = jnp.full((1, LANES), lam_init, F32)
        sub = diff_subln[l]

        q, cmp, slc, win, dq, dkv, gates = rms_mm(xp, g_attn_pre[l][None], w_proj, proj_splits, F32,
                                                  name="proj_prompt")
        comp = compress_rows(cmp, pe_rows, w1p, w2, rows_per_step=t)
        comp = comp.reshape(b, 2, NSA_KV_HEADS, nb, NSA_HD)
        kc = jnp.transpose(comp[:, 0], (0, 2, 1, 3)).reshape(b, nb, KV_W).astype(BF16)
        vct = jnp.transpose(comp[:, 1], (0, 1, 3, 2)).reshape(b, KV_W, nb).astype(BF16)

        def key_tiles(a):
            return a.astype(BF16).reshape(b, nqt, TK, a.shape[-1])

        def val_tiles_t(a):
            return jnp.transpose(a.astype(BF16).reshape(b, nqt, TK, a.shape[-1]), (0, 1, 3, 2))

        q5 = q.astype(BF16).reshape(b, nqt, TQ, NSA_KV_HEADS, NSA_GROUP, NSA_HD)
        wq_half = jnp.transpose(q5, (0, 3, 1, 5, 4, 2)).reshape(b, NSA_KV_HEADS, nqt, NSA_HD, NSA_GROUP * TQ)
        zeros_half = jnp.zeros_like(wq_half[:, 0])
        wq_n = jnp.stack([jnp.concatenate([wq_half[:, 0], zeros_half], axis=2),
                          jnp.concatenate([zeros_half, wq_half[:, 1]], axis=2)], axis=1)
        gt = jnp.transpose(gates.reshape(b, t, LANES)[:, :, :32], (0, 2, 1))
        front = ((0, 0), (WIN_TILES - 1, 0), (0, 0), (0, 0))
        o_nsa_t = nsa_prompt(wq_n, kc, vct,
                             key_tiles(slc[:, :KV_W]), val_tiles_t(slc[:, KV_W:]),
                             jnp.pad(key_tiles(win[:, :KV_W]), front), jnp.pad(val_tiles_t(win[:, KV_W:]), front),
                             gt, near_b, win_b, bcn)

        dq5 = dq.astype(BF16).reshape(b, nqt, TQ, DIFF_HEADS, 2, DIFF_QK)
        dq_t = jnp.transpose(dq5, (0, 3, 1, 4, 5, 2))
        zq = jnp.zeros_like(dq_t[:, :, :, 0])
        wq_d = jnp.concatenate([jnp.concatenate([dq_t[:, :, :, 0], zq], axis=-1),
                                jnp.concatenate([zq, dq_t[:, :, :, 1]], axis=-1)], axis=-2)
        kd = dkv[:, :DIFF_W].astype(BF16).reshape(b, nqt, TK, DIFF_HEADS, 2 * DIFF_QK)
        kd = jnp.transpose(kd, (0, 3, 1, 2, 4))
        vd = dkv[:, DIFF_W:].astype(BF16).reshape(b, nqt, TK, DIFF_HEADS, DIFF_VD)
        vd_t = jnp.transpose(vd, (0, 3, 1, 4, 2))
        o_diff_t = diff_prompt(wq_d, kd, vd_t, near_d, lam_vecs, sub[:, None], cst)

        o_cat = jnp.transpose(jnp.concatenate([o_nsa_t, o_diff_t], axis=1), (0, 2, 1)).reshape(b * t, dm)
        xp = mm_rms_res(o_cat, w_o_l, g_attn_post[l][None], xp, name="oproj_prompt")
        (hid,) = rms_mm(xp, g_mlp_pre[l][None], w_up_l, (w_up_l.shape[1],), BF16, act=True, name="mlp_up_prompt")
        xp = mm_rms_res(hid, w_down_l, g_mlp_post[l][None], xp, name="mlp_down_prompt")

        p_states[0].append(cmp.reshape(b, t, 2, NSA_KV_HEADS, NSA_HD))
        p_states[1].append(slc.reshape(b, t, 2, NSA_KV_HEADS, NSA_HD))
        p_states[2].append(dkv.reshape(b, t, 2, DIFF_HEADS, DIFF_VD))
        w_keep = min(WINDOW, t)
        p_states[3].append(win.reshape(b, t, 2, NSA_KV_HEADS, NSA_HD)[:, t - w_keep:])

        q, cmp, slc, win, dq, dkv, gates = rms_mm(xs, g_attn_pre[l][None], w_proj, proj_splits, F32,
                                                  name="proj_sample")
        page_ids = (page_table + l * n_pool).reshape(-1).astype(jnp.int32)
        npg_step = 4 * npg
        pe_t = jnp.concatenate([jnp.transpose(pe, (0, 2, 1))] * 2, axis=2)
        wt = jnp.transpose(w_cmp1[l].reshape(2, CMP_BLOCK, NSA_HD // 2, 2, -1), (0, 2, 3, 1, 4))
        w1d = jnp.concatenate([wt[:, :, 0], wt[:, :, 0], wt[:, :, 1], wt[:, :, 1]], axis=2).astype(BF16)
        comp = compress_pages(cmp_pages, page_ids, pe_t, w1d, w2, npg_step)
        nb_past = past_len // CMP_BLOCK
        comp = comp.reshape(nbatch // 4, 2, NSA_KV_HEADS, 2, 4, npg, NSA_HD)
        kcvc = jnp.transpose(comp, (0, 4, 1, 2, 6, 5, 3)).reshape(nbatch, 2 * KV_W, nb_past)
        kcvc = jnp.pad(kcvc, ((0, 0), (0, 0), (0, LANES - nb_past)))

        q4 = q.reshape(nbatch, NSA_KV_HEADS, NSA_GROUP, NSA_HD)
        zq4 = jnp.zeros_like(q4[:, 0])
        qs = jnp.concatenate([jnp.concatenate([q4[:, 0], zq4, zq4, zq4], axis=-1),
                              jnp.concatenate([zq4, q4[:, 1], zq4, zq4], axis=-1)], axis=1)
        dq4 = jnp.transpose(dq.reshape(nbatch, DIFF_HEADS, 2, DIFF_QK), (0, 2, 1, 3))
        eye_h = jnp.eye(DIFF_HEADS, dtype=F32)[None, :, :, None, None]
        eye_m = jnp.eye(2, dtype=F32)[:, None, None, :, None]
        qd = (dq4[:, :, :, None, None, :] * (eye_h * eye_m)).reshape(nbatch, 2 * DIFF_HEADS, DIFF_W)
        gate8 = jnp.pad(gates[:, :N_GATES].reshape(nbatch, NSA_HEADS, 3), ((0, 0), (0, 0), (0, LANES - 3)))
        sub_row = jnp.tile(sub, DIFF_HEADS)[None]
        o_nsa8, o_diff = sample_mixer(
            page_ids, slc_pages, diff_pages, qs, qd, gate8, kcvc,
            slc[:, None], win[:, None], dkv[:, None], win_state[l],
            bias_c, bias_s, bias_w, bias_d, bnew, expand, lam_vecs, sub_row, cst, npg)
        o8 = o_nsa8.reshape(nbatch, NSA_KV_HEADS, NSA_GROUP, 4, NSA_HD)
        o_nsa = jnp.stack([o8[:, 0, :, 2], o8[:, 1, :, 3]], axis=1).reshape(nbatch, NSA_W)
        o_cat = jnp.concatenate([o_nsa, o_diff.reshape(nbatch, DIFF_W)], axis=1).astype(BF16)
        xs = mm_rms_res(o_cat, w_o_l, g_attn_post[l][None], xs, name="oproj_sample")
        (hid,) = rms_mm(xs, g_mlp_pre[l][None], w_up_l, (w_up_l.shape[1],), BF16, act=True, name="mlp_up_sample")
        xs = mm_rms_res(hid, w_down_l, g_mlp_post[l][None], xs, name="mlp_down_sample")

        s_states[0].append(cmp.reshape(nbatch, 1, 2, NSA_KV_HEADS, NSA_HD))
        s_states[1].append(slc.reshape(nbatch, 1, 2, NSA_KV_HEADS, NSA_HD))
        s_states[2].append(dkv.reshape(nbatch, 1, 2, DIFF_HEADS, DIFF_VD))
        win_new = win.reshape(nbatch, 1, 2, NSA_KV_HEADS, NSA_HD)
        s_states[3].append(jnp.concatenate([state_win_kv[l][:, 1:], win_new], axis=1))

    return (xp.reshape(b, t, dm), xs.reshape(nbatch, 1, dm),
            jnp.stack(p_states[0]), jnp.stack(p_states[1]), jnp.stack(p_states[2]), jnp.stack(p_states[3]),
            jnp.stack(s_states[0]), jnp.stack(s_states[1]), jnp.stack(s_states[2]), jnp.stack(s_states[3]))
```

```python
import functools
import math

import jax
import jax.numpy as jnp
from jax import lax
from jax.experimental import pallas as pl
from jax.experimental.pallas import tpu as pltpu

NSA_HEADS = 8
NSA_KV_HEADS = 2
NSA_GROUP = NSA_HEADS // NSA_KV_HEADS
NSA_HD = 64
CMP_BLOCK = 64
N_SELECT = 16
WINDOW = 512
DIFF_HEADS = 4
DIFF_QK = 64
DIFF_VD = 2 * DIFF_QK
N_BUCKETS = 32
MAX_DISTANCE = 128
PAGE_SIZE = 128
EPS = 1e-6
NEG = -1e30

NSA_W = NSA_HEADS * NSA_HD
DIFF_W = DIFF_HEADS * DIFF_VD
KV_W = NSA_KV_HEADS * NSA_HD
N_GATES = NSA_HEADS * 3
QK_SCALE = NSA_HD ** -0.5

LANES = 128
TQ = 128
TK = 128
ROW_TILE = 512
FAR_TILES = 8
WIN_TILES = WINDOW // TK + 1
BLK_PER_TILE = TK // CMP_BLOCK
DIFF_HPS = 2
VMEM_LIMIT_BYTES = 56 * 1024 * 1024

F32 = jnp.float32
BF16 = jnp.bfloat16


def _cparams(*sem):
    return pltpu.CompilerParams(dimension_semantics=sem, vmem_limit_bytes=VMEM_LIMIT_BYTES)


def _dot(a, b):
    return jnp.dot(a, b, preferred_element_type=F32)


def _dot_nt(a, b):
    return lax.dot_general(a, b, (((1,), (1,)), ((), ())), preferred_element_type=F32)


def _dot_tn(a, b):
    return lax.dot_general(a, b, (((0,), (0,)), ((), ())), preferred_element_type=F32)


def _rms(x, g):
    ms = jnp.mean(x * x, axis=-1, keepdims=True)
    return (x * lax.rsqrt(ms + EPS)) * g


PROJ_NN_COLS = 2 * KV_W + 2 * DIFF_W
PROJ_NT_ROWS = NSA_W + 32 + 3 * 2 * KV_W + 2 * DIFF_W


def _proj_prompt_kernel(x_ref, g_ref, wn_ref, wt_ref, ks_ref, kw_ref, kd_ref, dkv_ref, wqn_ref, gt_ref,
                        cmpt_ref, slct_ref, wint_ref, vst_ref, vwt_ref, wqd_ref, vdt_ref):
    h = _rms(x_ref[...], g_ref[...]).astype(BF16)
    n_tiles = h.shape[0] // TQ
    ks_ref[...] = _dot(h, wn_ref[:, 0:KV_W]).astype(BF16)
    kw_ref[...] = _dot(h, wn_ref[:, KV_W:2 * KV_W]).astype(BF16)
    ykv = _dot(h, wn_ref[:, 2 * KV_W:])
    dkv_ref[...] = ykv
    for hh in range(DIFF_HEADS):
        kd_ref[hh] = ykv[:, hh * DIFF_VD:(hh + 1) * DIFF_VD].astype(BF16)

    def rows(r0, n):
        return _dot_nt(wt_ref[r0:r0 + n, :], h)

    r = 0
    yq = rows(r, NSA_W)
    r += NSA_W
    gt_ref[...] = rows(r, 32)
    r += 32
    cmpt_ref[...] = rows(r, 2 * KV_W)
    r += 2 * KV_W
    ys = rows(r, 2 * KV_W)
    r += 2 * KV_W
    yw = rows(r, 2 * KV_W)
    r += 2 * KV_W
    ydq = rows(r, DIFF_W)
    r += DIFF_W
    ydv = rows(r, DIFF_W)
    slct_ref[...] = ys
    wint_ref[...] = yw
    zero_q = jnp.zeros((NSA_HD, NSA_GROUP * TQ), BF16)
    zero_d = jnp.zeros((DIFF_QK, TQ), BF16)
    for qt in range(n_tiles):
        ls = slice(qt * TQ, (qt + 1) * TQ)
        vst_ref[qt] = ys[KV_W:, ls].astype(BF16)
        vwt_ref[qt] = yw[KV_W:, ls].astype(BF16)
        for kvh in range(NSA_KV_HEADS):
            base = kvh * NSA_GROUP * NSA_HD
            wq = jnp.concatenate([yq[base + g * NSA_HD:base + (g + 1) * NSA_HD, ls] for g in range(NSA_GROUP)],
                                 axis=1).astype(BF16)
            wqn_ref[kvh, qt] = jnp.concatenate([wq, zero_q] if kvh == 0 else [zero_q, wq], axis=0)
        for hh in range(DIFF_HEADS):
            q1 = ydq[hh * DIFF_VD:hh * DIFF_VD + DIFF_QK, ls].astype(BF16)
            q2 = ydq[hh * DIFF_VD + DIFF_QK:(hh + 1) * DIFF_VD, ls].astype(BF16)
            wqd_ref[hh, qt] = jnp.concatenate([jnp.concatenate([q1, zero_d], axis=1),
                                               jnp.concatenate([zero_d, q2], axis=1)], axis=0)
            vdt_ref[hh, qt] = ydv[hh * DIFF_VD:(hh + 1) * DIFF_VD, ls].astype(BF16)


def proj_prompt(x, g, w_nn, w_nt, b, t):
    dm = x.shape[1]
    tm = ROW_TILE
    spb = t // tm
    tps = tm // TQ
    nqt = t // TQ
    nl_n = NSA_GROUP * TQ
    f32, bf = F32, BF16
    out_shape = [
        jax.ShapeDtypeStruct((b * t, KV_W), bf),
        jax.ShapeDtypeStruct((b * t, KV_W), bf),
        jax.ShapeDtypeStruct((b, DIFF_HEADS, t, DIFF_VD), bf),
        jax.ShapeDtypeStruct((b * t, 2 * DIFF_W), f32),
        jax.ShapeDtypeStruct((b, NSA_KV_HEADS, nqt, LANES, nl_n), bf),
        jax.ShapeDtypeStruct((b, 32, t), f32),
        jax.ShapeDtypeStruct((b, 2 * KV_W, t), f32),
        jax.ShapeDtypeStruct((b, 2 * KV_W, t), f32),
        jax.ShapeDtypeStruct((b, 2 * KV_W, t), f32),
        jax.ShapeDtypeStruct((b, nqt, KV_W, TK), bf),
        jax.ShapeDtypeStruct((b, nqt, KV_W, TK), bf),
        jax.ShapeDtypeStruct((b, DIFF_HEADS, nqt, LANES, 2 * TQ), bf),
        jax.ShapeDtypeStruct((b, DIFF_HEADS, nqt, DIFF_VD, TK), bf),
    ]
    rowblk = lambda w: pl.BlockSpec((tm, w), lambda bi, si: (bi * spb + si, 0))
    featblk = lambda n: pl.BlockSpec((None, n, tm), lambda bi, si: (bi, 0, si))
    out_specs = [
        rowblk(KV_W), rowblk(KV_W),
        pl.BlockSpec((None, DIFF_HEADS, tm, DIFF_VD), lambda bi, si: (bi, 0, si, 0)),
        rowblk(2 * DIFF_W),
        pl.BlockSpec((None, NSA_KV_HEADS, tps, LANES, nl_n), lambda bi, si: (bi, 0, si, 0, 0)),
        featblk(32), featblk(2 * KV_W), featblk(2 * KV_W), featblk(2 * KV_W),
        pl.BlockSpec((None, tps, KV_W, TK), lambda bi, si: (bi, si, 0, 0)),
        pl.BlockSpec((None, tps, KV_W, TK), lambda bi, si: (bi, si, 0, 0)),
        pl.BlockSpec((None, DIFF_HEADS, tps, LANES, 2 * TQ), lambda bi, si: (bi, 0, si, 0, 0)),
        pl.BlockSpec((None, DIFF_HEADS, tps, DIFF_VD, TK), lambda bi, si: (bi, 0, si, 0, 0)),
    ]
    return pl.pallas_call(
        _proj_prompt_kernel,
        grid=(b, spb),
        in_specs=[pl.BlockSpec((tm, dm), lambda bi, si: (bi * spb + si, 0)),
                  pl.BlockSpec((1, dm), lambda bi, si: (0, 0)),
                  pl.BlockSpec(w_nn.shape, lambda bi, si: (0, 0)),
                  pl.BlockSpec(w_nt.shape, lambda bi, si: (0, 0))],
        out_specs=out_specs,
        out_shape=out_shape,
        compiler_params=_cparams("parallel", "parallel"),
        name="proj_prompt",
    )(x, g, w_nn, w_nt)


def _rms_mm_kernel(x_ref, g_ref, w_ref, *o_refs, splits, act):
    x = x_ref[...]
    ms = jnp.mean(x * x, axis=-1, keepdims=True)
    h = ((x * lax.rsqrt(ms + EPS)) * g_ref[...]).astype(BF16)
    off = 0
    for o_ref, n in zip(o_refs, splits):
        y = _dot(h, w_ref[:, off:off + n])
        if act:
            y = jnp.square(jnp.maximum(y, 0.0))
        o_ref[...] = y.astype(o_ref.dtype)
        off += n


def rms_mm(x, g, w, splits, out_dtype, act=False, name="rms_mm"):
    m, k = x.shape
    tm = min(ROW_TILE, m)
    n = sum(splits)
    return pl.pallas_call(
        functools.partial(_rms_mm_kernel, splits=tuple(splits), act=act),
        grid=(m // tm,),
        in_specs=[pl.BlockSpec((tm, k), lambda i: (i, 0)),
                  pl.BlockSpec((1, k), lambda i: (0, 0)),
                  pl.BlockSpec((k, n), lambda i: (0, 0))],
        out_specs=[pl.BlockSpec((tm, s), lambda i: (i, 0)) for s in splits],
        out_shape=[jax.ShapeDtypeStruct((m, s), out_dtype) for s in splits],
        compiler_params=_cparams("parallel"),
        name=name,
    )(x, g, w)


def _mm_rms_res_kernel(a_ref, w_ref, g_ref, r_ref, o_ref):
    y = _dot(a_ref[...], w_ref[...])
    ms = jnp.mean(y * y, axis=-1, keepdims=True)
    o_ref[...] = r_ref[...] + (y * lax.rsqrt(ms + EPS)) * g_ref[...]


def _oproj_t_kernel(on_ref, od_ref, w_ref, g_ref, r_ref, o_ref):
    n_nsa = on_ref.shape[0]
    y = _dot_tn(on_ref[...], w_ref[0:n_nsa, :]) + _dot_tn(od_ref[...], w_ref[n_nsa:, :])
    o_ref[...] = r_ref[...] + _rms(y, g_ref[...])


def oproj_t(o_nsa_t, o_diff_t, w, g, res):
    b, n_nsa, t = o_nsa_t.shape
    n_diff = o_diff_t.shape[1]
    n = w.shape[1]
    tm = ROW_TILE
    spb = t // tm
    return pl.pallas_call(
        _oproj_t_kernel,
        grid=(b, spb),
        in_specs=[pl.BlockSpec((None, n_nsa, tm), lambda bi, si: (bi, 0, si)),
                  pl.BlockSpec((None, n_diff, tm), lambda bi, si: (bi, 0, si)),
                  pl.BlockSpec(w.shape, lambda bi, si: (0, 0)),
                  pl.BlockSpec((1, n), lambda bi, si: (0, 0)),
                  pl.BlockSpec((tm, n), lambda bi, si: (bi * spb + si, 0))],
        out_specs=pl.BlockSpec((tm, n), lambda bi, si: (bi * spb + si, 0)),
        out_shape=jax.ShapeDtypeStruct((b * t, n), F32),
        compiler_params=_cparams("parallel", "parallel"),
        name="oproj_prompt",
    )(o_nsa_t, o_diff_t, w, g, res)


def mm_rms_res(a, w, g, res, name="mm_rms_res"):
    m, k = a.shape
    n = w.shape[1]
    tm = min(ROW_TILE, m)
    return pl.pallas_call(
        _mm_rms_res_kernel,
        grid=(m // tm,),
        in_specs=[pl.BlockSpec((tm, k), lambda i: (i, 0)),
                  pl.BlockSpec((k, n), lambda i: (0, 0)),
                  pl.BlockSpec((1, n), lambda i: (0, 0)),
                  pl.BlockSpec((tm, n), lambda i: (i, 0))],
        out_specs=pl.BlockSpec((tm, n), lambda i: (i, 0)),
        out_shape=jax.ShapeDtypeStruct((m, n), F32),
        compiler_params=_cparams("parallel"),
        name=name,
    )(a, w, g, res)


def _compress_body(x_refs, pe_ref, w1_ref, w2_ref, o_ref, acc_ref, nb):
    lane = lax.broadcasted_iota(jnp.int32, (nb, 2 * LANES), 1)
    head0 = (lane % LANES) < NSA_HD
    acc_ref[...] = jnp.zeros_like(acc_ref)

    def step(tp, carry):
        t0 = 2 * tp
        for c in range(2):
            a0 = x_refs[c][pl.ds(t0, nb, stride=CMP_BLOCK), :] + pe_ref[c, pl.ds(t0, 1), :]
            a1 = x_refs[c][pl.ds(t0 + 1, nb, stride=CMP_BLOCK), :] + pe_ref[c, pl.ds(t0 + 1, 1), :]
            ll = jnp.concatenate([a0, a1], axis=1)
            l2 = jnp.concatenate([jnp.where(head0, ll, 0.0), jnp.where(head0, 0.0, ll)], axis=0).astype(BF16)
            acc_ref[c] += _dot(l2, w1_ref[c, tp])
        return carry

    lax.fori_loop(0, CMP_BLOCK // 2, step, 0)
    for c in range(2):
        hid = jax.nn.gelu(acc_ref[c]).astype(BF16)
        o_ref[c] = _dot(hid, w2_ref[c])


def _compress_rows_kernel(xk_ref, xv_ref, pe_ref, w1_ref, w2_ref, o_ref, acc_ref, *, nb):
    _compress_body((xk_ref, xv_ref), pe_ref, w1_ref, w2_ref, o_ref, acc_ref, nb)


def _compress_pages_kernel(pt_ref, *refs, npg):
    del pt_ref
    page_refs = refs[:npg]
    pe_ref, w1_ref, w2_ref, o_ref, x_ref, acc_ref = refs[npg:]
    n_ch = 2 * NSA_KV_HEADS
    for j in range(npg):
        for ch in range(n_ch):
            x_ref[ch, j * NSA_HD:(j + 1) * NSA_HD, :] = page_refs[j][ch * NSA_HD:(ch + 1) * NSA_HD, :]
    lane = lax.broadcasted_iota(jnp.int32, (npg, 2 * LANES), 1)
    even = (lane % LANES) < CMP_BLOCK
    acc_ref[...] = jnp.zeros_like(acc_ref)

    def step(dp, carry):
        d0 = 2 * dp
        for c in range(2):
            parts = []
            for h in range(NSA_KV_HEADS):
                xr = x_ref.at[c * NSA_KV_HEADS + h]
                a0 = xr[pl.ds(d0, npg, stride=NSA_HD), :] + pe_ref[c, pl.ds(d0, 1), :]
                a1 = xr[pl.ds(d0 + 1, npg, stride=NSA_HD), :] + pe_ref[c, pl.ds(d0 + 1, 1), :]
                ll = jnp.concatenate([a0, a1], axis=1)
                parts += [jnp.where(even, ll, 0.0), jnp.where(even, 0.0, ll)]
            acc_ref[c] += _dot(jnp.concatenate(parts, axis=0).astype(BF16), w1_ref[c, dp])
        return carry

    lax.fori_loop(0, NSA_HD // 2, step, 0)
    for c in range(2):
        hid = jax.nn.gelu(acc_ref[c]).astype(BF16)
        o_ref[c] = _dot(hid, w2_ref[c])


def compress_rows(x, pe_rows, w1p, w2, rows_per_step):
    r = x.shape[0]
    steps = r // rows_per_step
    nb = rows_per_step // CMP_BLOCK
    return pl.pallas_call(
        functools.partial(_compress_rows_kernel, nb=nb),
        grid=(steps,),
        in_specs=[pl.BlockSpec((rows_per_step, LANES), lambda i: (i, 0)),
                  pl.BlockSpec((rows_per_step, LANES), lambda i: (i, 1)),
                  pl.BlockSpec(pe_rows.shape, lambda i: (0, 0, 0)),
                  pl.BlockSpec(w1p.shape, lambda i: (0, 0, 0, 0)),
                  pl.BlockSpec(w2.shape, lambda i: (0, 0, 0))],
        out_specs=pl.BlockSpec((None, 2, 2 * nb, NSA_HD), lambda i: (i, 0, 0, 0)),
        out_shape=jax.ShapeDtypeStruct((steps, 2, 2 * nb, NSA_HD), F32),
        scratch_shapes=[pltpu.VMEM((2, 2 * nb, 2 * LANES), F32)],
        compiler_params=_cparams("parallel"),
        name="compress_rows",
    )(x, x, pe_rows, w1p, w2)


def compress_pages(pages, page_ids, pe_t, w1d, w2, npg, name):
    steps = page_ids.shape[0] // npg
    n_rows = 2 * NSA_KV_HEADS * npg
    pages_per_seq = pages.shape[2] // PAGE_SIZE

    def page_spec(j):
        def index(i, pt):
            pid = pt[i * npg + j]
            return (pid // pages_per_seq, 0, pid % pages_per_seq)
        return pl.BlockSpec((None, pages.shape[1], PAGE_SIZE), index)

    grid_spec = pltpu.PrefetchScalarGridSpec(
        num_scalar_prefetch=1,
        grid=(steps,),
        in_specs=[page_spec(j) for j in range(npg)] + [
            pl.BlockSpec(pe_t.shape, lambda i, pt: (0, 0, 0)),
            pl.BlockSpec(w1d.shape, lambda i, pt: (0, 0, 0, 0)),
            pl.BlockSpec(w2.shape, lambda i, pt: (0, 0, 0))],
        out_specs=pl.BlockSpec((None, 2, n_rows, NSA_HD), lambda i, pt: (i, 0, 0, 0)),
        scratch_shapes=[pltpu.VMEM((2 * NSA_KV_HEADS, npg * NSA_HD, LANES), F32),
                        pltpu.VMEM((2, n_rows, 2 * LANES), F32)],
    )
    return pl.pallas_call(
        functools.partial(_compress_pages_kernel, npg=npg),
        grid_spec=grid_spec,
        out_shape=jax.ShapeDtypeStruct((steps, 2, n_rows, NSA_HD), F32),
        compiler_params=_cparams("arbitrary"),
        name=name,
    )(page_ids, *([pages] * npg), pe_t, w1d, w2)


def _softmax_update(carry, s, vt_parts):
    m, l, acc = carry
    m_new = jnp.maximum(m, jnp.max(s, axis=0, keepdims=True))
    alpha = jnp.exp(m - m_new)
    p = jnp.exp(s - m_new)
    l = alpha * l + jnp.sum(p, axis=0, keepdims=True)
    pb = p.astype(BF16)
    pv = _dot(vt_parts[0], pb[0:TK])
    for j in range(1, len(vt_parts)):
        pv = pv + _dot(vt_parts[j], pb[j * TK:(j + 1) * TK])
    return m_new, l, alpha * acc + pv


def _init_carry(dv, nl):
    return (jnp.full((1, nl), NEG, F32), jnp.zeros((1, nl), F32), jnp.zeros((dv, nl), F32))


def _add_rows(s, rows, rows_per):
    n = rows.shape[0]
    return jnp.concatenate([s[i * rows_per:(i + 1) * rows_per] + rows[i:i + 1] for i in range(n)], axis=0)


def _valid_rows(conds, nl, reps=1):
    rows = []
    for cond in conds:
        rows.extend([jnp.full((1, nl), jnp.where(cond, 0.0, NEG), F32)] * reps)
    return jnp.concatenate(rows, axis=0)


def _nsa_prompt_kernel(wq_ref, kc_ref, vct_ref, ks_ref, vst_ref, kw_ref, vwt_ref, gt_ref,
                       near_ref, winb_ref, bcn_ref, o_ref, sel_ref, *, nb, n_top):
    qi = pl.program_id(1)
    nl = NSA_GROUP * TQ
    brow = lax.broadcasted_iota(jnp.int32, (nb, nl), 0)
    rqb = lax.broadcasted_iota(jnp.int32, (nb, nl), 1) % TQ
    cur = BLK_PER_TILE * qi + (rqb >= CMP_BLOCK).astype(jnp.int32)
    jr = lax.broadcasted_iota(jnp.int32, (nb, TQ), 0).astype(F32)
    cur_q = (BLK_PER_TILE * qi
             + (lax.broadcasted_iota(jnp.int32, (nb, TQ), 1) >= CMP_BLOCK).astype(jnp.int32)).astype(F32)
    hs = [slice(kvh * NSA_HD, (kvh + 1) * NSA_HD) for kvh in range(NSA_KV_HEADS)]
    wqs, o_cs = [], []
    for kvh in range(NSA_KV_HEADS):
        wq = (wq_ref[kvh].astype(F32) * QK_SCALE).astype(BF16)
        wqs.append(wq)
        jrel = brow - (BLK_PER_TILE * qi - 2)
        bias_c = jnp.zeros((nb, nl), F32)
        for k in range(3):
            bias_c = jnp.where(jrel == k, bcn_ref[kvh, k:k + 1, :], bias_c)
        mask_c = brow < cur
        sc = jnp.where(mask_c, _dot(kc_ref[...], wq) + bias_c, NEG)
        mc = jnp.max(sc, axis=0, keepdims=True)
        pc = jnp.where(mask_c, jnp.exp(sc - mc), 0.0)
        den = jnp.sum(pc, axis=0, keepdims=True)
        pc = pc / jnp.where(den > 0, den, 1.0)
        o_cs.append(_dot(vct_ref[hs[kvh], :], pc.astype(BF16)))
        imp = pc[:, 0:TQ]
        for g in range(1, NSA_GROUP):
            imp = imp + pc[:, g * TQ:(g + 1) * TQ]
        imp = jnp.where(jr < cur_q, imp, -1.0)
        v = imp
        sel = jr == cur_q
        for _ in range(n_top):
            mx = jnp.max(v, axis=0, keepdims=True)
            idx = jnp.min(jnp.where(v == mx, jr, float(nb)), axis=0, keepdims=True)
            hit = jr == idx
            sel = sel | (hit & (mx >= 0.0))
            v = jnp.where(hit, -2.0, v)
        sel_add = jnp.where(sel, 0.0, NEG)
        sel_ref[kvh] = jnp.concatenate([sel_add] * NSA_GROUP, axis=1)

    def far_chunk(c, carries, limit):
        t0 = c * FAR_TILES
        k_rows = ks_ref[pl.ds(t0, FAR_TILES)].reshape(FAR_TILES * TK, LANES)
        out = []
        for kvh in range(NSA_KV_HEADS):
            rows = sel_ref[kvh, pl.ds(pl.multiple_of(t0 * BLK_PER_TILE, 8), FAR_TILES * BLK_PER_TILE), :]
            if limit is not None:
                rows = rows + _valid_rows([t0 + j <= limit for j in range(FAR_TILES)], nl, BLK_PER_TILE)
            s = _add_rows(_dot(k_rows, wqs[kvh]), rows, CMP_BLOCK)
            vts = [vst_ref[t0 + j, hs[kvh], :] for j in range(FAR_TILES)]
            out.append(_softmax_update(carries[kvh], s, vts))
        return tuple(out)

    n_full = jnp.maximum(qi - 1, 0) // FAR_TILES
    carries = tuple(_init_carry(NSA_HD, nl) for _ in range(NSA_KV_HEADS))
    carries = lax.fori_loop(0, n_full, lambda c, cr: far_chunk(c, cr, None), carries)
    carries = far_chunk(n_full, carries, qi - 2)
    kt1 = jnp.maximum(qi - 1, 0)
    k_near = jnp.concatenate([ks_ref[kt1], ks_ref[qi]], axis=0)
    win_tiles = [jnp.maximum(qi - o, 0) for o in range(WIN_TILES - 1, -1, -1)]
    k_win = jnp.concatenate([kw_ref[kt] for kt in win_tiles], axis=0)
    prev_ok = _valid_rows([qi >= 1], nl)
    win_ok = _valid_rows([qi >= o for o in range(WIN_TILES - 1, -1, -1)], nl)
    for kvh in range(NSA_KV_HEADS):
        rows = jnp.concatenate(
            [sel_ref[kvh, pl.ds(BLK_PER_TILE * kt1 + i, 1), :] + prev_ok for i in range(BLK_PER_TILE)]
            + [sel_ref[kvh, pl.ds(BLK_PER_TILE * qi + i, 1), :] for i in range(BLK_PER_TILE)], axis=0)
        s = _add_rows(_dot(k_near, wqs[kvh]) + near_ref[kvh], rows, CMP_BLOCK)
        _, l_s, a_s = _softmax_update(carries[kvh], s, [vst_ref[kt1, hs[kvh], :], vst_ref[qi, hs[kvh], :]])
        o_s = a_s / l_s
        s = _add_rows(_dot(k_win, wqs[kvh]) + winb_ref[kvh], win_ok, TK)
        _, l_w, a_w = _softmax_update(_init_carry(NSA_HD, nl), s,
                                      [vwt_ref[kt, hs[kvh], :] for kt in win_tiles])
        o_w = a_w / l_w
        for g in range(NSA_GROUP):
            h = kvh * NSA_GROUP + g
            gs = jax.nn.sigmoid(gt_ref[3 * h:3 * h + 3, :])
            ls = slice(g * TQ, (g + 1) * TQ)
            o = gs[0:1] * o_cs[kvh][:, ls] + gs[1:2] * o_s[:, ls] + gs[2:3] * o_w[:, ls]
            o_ref[h * NSA_HD:(h + 1) * NSA_HD, :] = o.astype(o_ref.dtype)


def nsa_prompt(wq, kc, vct, ks, vst, kw, vwt, gt, near_b, win_b, bcn):
    b, _, nqt, _, nl = wq.shape
    nb = kc.shape[1]
    n_top = min(N_SELECT - 1, nb)
    t = nqt * TQ
    assert nqt % FAR_TILES == 0
    full = lambda shape: pl.BlockSpec((None,) + shape, lambda bi, qi: (bi,) + (0,) * len(shape))
    const = lambda arr: pl.BlockSpec(arr.shape, lambda bi, qi: (0,) * arr.ndim)
    return pl.pallas_call(
        functools.partial(_nsa_prompt_kernel, nb=nb, n_top=n_top),
        grid=(b, nqt),
        in_specs=[pl.BlockSpec((None, NSA_KV_HEADS, None, LANES, nl), lambda bi, qi: (bi, 0, qi, 0, 0)),
                  full((nb, LANES)), full((LANES, nb)),
                  full((nqt, TK, LANES)), full((nqt, LANES, TK)),
                  full((nqt, TK, LANES)), full((nqt, LANES, TK)),
                  pl.BlockSpec((None, 32, TQ), lambda bi, qi: (bi, 0, qi)),
                  const(near_b), const(win_b), const(bcn)],
        out_specs=pl.BlockSpec((None, NSA_W, TQ), lambda bi, qi: (bi, 0, qi)),
        out_shape=jax.ShapeDtypeStruct((b, NSA_W, t), BF16),
        scratch_shapes=[pltpu.VMEM((NSA_KV_HEADS, nb, nl), F32)],
        compiler_params=_cparams("parallel", "arbitrary"),
        name="nsa_prompt",
    )(wq, kc, vct, ks, vst, kw, vwt, gt, near_b, win_b, bcn)


def _lambda_value(lam_ref, cst_ref):
    lv = lam_ref[...]
    s1 = jnp.sum(lv[0:1] * lv[1:2], axis=1, keepdims=True)
    s2 = jnp.sum(lv[2:3] * lv[3:4], axis=1, keepdims=True)
    lam_init = cst_ref[0:1, 0:1]
    return jnp.exp(s1) - jnp.exp(s2) + lam_init, lam_init


def _diff_prompt_kernel(wq_ref, k_ref, vt_ref, near_ref, lam_ref, sub_ref, cst_ref, o_ref):
    qi = pl.program_id(2)
    nl = 2 * TQ
    wqs = [(wq_ref[h].astype(F32) * QK_SCALE).astype(BF16) for h in range(DIFF_HPS)]

    def far_chunk(c, carries, limit):
        t0 = c * FAR_TILES
        out = []
        for h in range(DIFF_HPS):
            s = _dot(k_ref[h, pl.ds(t0, FAR_TILES)].reshape(FAR_TILES * TK, LANES), wqs[h])
            if limit is not None:
                s = _add_rows(s, _valid_rows([t0 + j <= limit for j in range(FAR_TILES)], nl), TK)
            out.append(_softmax_update(carries[h], s, [vt_ref[h, t0 + j] for j in range(FAR_TILES)]))
        return tuple(out)

    n_full = jnp.maximum(qi - 1, 0) // FAR_TILES
    carries = tuple(_init_carry(DIFF_VD, nl) for _ in range(DIFF_HPS))
    carries = lax.fori_loop(0, n_full, lambda c, cr: far_chunk(c, cr, None), carries)
    carries = far_chunk(n_full, carries, qi - 2)
    kt1 = jnp.maximum(qi - 1, 0)
    lam, lam_init = _lambda_value(lam_ref, cst_ref)
    near_ok = _valid_rows([qi >= 1, True], nl)
    for h in range(DIFF_HPS):
        k_near = jnp.concatenate([k_ref[h, kt1], k_ref[h, qi]], axis=0)
        s = _add_rows(_dot(k_near, wqs[h]) + near_ref[h], near_ok, TK)
        _, l, acc = _softmax_update(carries[h], s, [vt_ref[h, kt1], vt_ref[h, qi]])
        a = acc / l
        o = a[:, :TQ] - lam * a[:, TQ:]
        ms = jnp.mean(o * o, axis=0, keepdims=True)
        y = ((o * lax.rsqrt(ms + EPS)) * sub_ref[...]) * (1.0 - lam_init)
        o_ref[h * DIFF_VD:(h + 1) * DIFF_VD, :] = y.astype(o_ref.dtype)


def diff_prompt(wq, k, vt, near_b, lam_vecs, sub_col, cst):
    b, nh, nqt, _, nl = wq.shape
    t = nqt * TQ
    assert nqt % FAR_TILES == 0 and nh % DIFF_HPS == 0
    return pl.pallas_call(
        _diff_prompt_kernel,
        grid=(b, nh // DIFF_HPS, nqt),
        in_specs=[pl.BlockSpec((None, DIFF_HPS, None, LANES, nl), lambda bi, h, qi: (bi, h, qi, 0, 0)),
                  pl.BlockSpec((None, DIFF_HPS, nqt, TK, LANES), lambda bi, h, qi: (bi, h, 0, 0, 0)),
                  pl.BlockSpec((None, DIFF_HPS, nqt, DIFF_VD, TK), lambda bi, h, qi: (bi, h, 0, 0, 0)),
                  pl.BlockSpec((DIFF_HPS, 2 * TK, nl), lambda bi, h, qi: (h, 0, 0)),
                  pl.BlockSpec(lam_vecs.shape, lambda bi, h, qi: (0, 0)),
                  pl.BlockSpec(sub_col.shape, lambda bi, h, qi: (0, 0)),
                  pl.BlockSpec(cst.shape, lambda bi, h, qi: (0, 0))],
        out_specs=pl.BlockSpec((None, DIFF_HPS * DIFF_VD, TQ), lambda bi, h, qi: (bi, h, qi)),
        out_shape=jax.ShapeDtypeStruct((b, DIFF_W, t), BF16),
        compiler_params=_cparams("parallel", "parallel", "arbitrary"),
        name="diff_prompt",
    )(wq, k, vt, near_b, lam_vecs, sub_col, cst)


def _round_bf16(x):
    return x.astype(BF16).astype(F32)


def _sample_kernel(pt_ref, *refs, npg, n_top, nb_past):
    del pt_ref
    slc_refs = refs[:npg]
    diff_refs = refs[npg:2 * npg]
    (qs_ref, qd_ref, gate_ref, kcvc_ref, slc_new_ref, win_new_ref, dkv_new_ref, win_ref,
     bias_c_ref, bias_s_ref, bias_w_ref, bias_d_ref, bnew_ref, expand_ref,
     lam_ref, sub_ref, cst_ref, o_nsa_ref, o_diff_ref) = refs[2 * npg:]
    past = npg * PAGE_SIZE

    qs_f = qs_ref[...] * QK_SCALE
    qs = qs_f.astype(BF16)
    kcvc = kcvc_ref[...].astype(BF16)
    sc = _dot(qs, kcvc) + bias_c_ref[...]
    mc = jnp.max(sc, axis=1, keepdims=True)
    ec = jnp.exp(sc - mc)
    pc = ec / jnp.sum(ec, axis=1, keepdims=True)
    o_c = _dot_nt(pc.astype(BF16), kcvc)
    ri = lax.broadcasted_iota(jnp.int32, (LANES, LANES), 0)
    ci = lax.broadcasted_iota(jnp.int32, (LANES, LANES), 1)
    sel_rows = []
    for kvh in range(NSA_KV_HEADS):
        imp = jnp.sum(pc[kvh * NSA_GROUP:(kvh + 1) * NSA_GROUP], axis=0, keepdims=True)
        rmat = jnp.broadcast_to(imp, (LANES, LANES))
        cmat = rmat.T
        beats = ((cmat > rmat) | ((cmat == rmat) & (ri < ci))) & (ri < nb_past)
        rank = jnp.sum(beats.astype(F32), axis=0, keepdims=True)
        sel_rows.append(jnp.where((rank < n_top) & (ci[0:1] < nb_past), 1.0, 0.0))
    hrow = lax.broadcasted_iota(jnp.int32, (NSA_HEADS, LANES), 0)
    sel8 = jnp.where(hrow < NSA_GROUP, sel_rows[0], sel_rows[1]).astype(BF16)
    sel_mask = _dot(sel8, expand_ref[...]) > 0.5
    s_parts = [_dot(qs, slc_refs[j][...].astype(BF16)) for j in range(npg)]
    s = jnp.where(sel_mask, jnp.concatenate(s_parts, axis=1) + bias_s_ref[...], NEG)
    new_s = _round_bf16(slc_new_ref[...])
    qs_r = qs.astype(F32)
    s_new = jnp.sum(qs_r * new_s, axis=1, keepdims=True) + bnew_ref[:, 0:1]
    m = jnp.maximum(jnp.max(s, axis=1, keepdims=True), s_new)
    pf = jnp.exp(s - m)
    p_new = jnp.exp(s_new - m)
    den = jnp.sum(pf, axis=1, keepdims=True) + p_new
    p = pf.astype(BF16)
    acc = _round_bf16(p_new) * new_s
    for j in range(npg):
        acc = acc + _dot_nt(p[:, j * PAGE_SIZE:(j + 1) * PAGE_SIZE], slc_refs[j][...].astype(BF16))
    o_s = acc / den
    wk = win_ref[...].astype(BF16)
    wl = lax.broadcasted_iota(jnp.int32, (NSA_HEADS, wk.shape[1]), 1)
    s = jnp.where(wl >= 1, _dot(qs, wk) + bias_w_ref[...], NEG)
    new_w = _round_bf16(win_new_ref[...])
    s_new = jnp.sum(qs_r * new_w, axis=1, keepdims=True) + bnew_ref[:, 0:1]
    m = jnp.maximum(jnp.max(s, axis=1, keepdims=True), s_new)
    pf = jnp.exp(s - m)
    p_new = jnp.exp(s_new - m)
    den = jnp.sum(pf, axis=1, keepdims=True) + p_new
    p = pf.astype(BF16)
    o_w = (_dot_nt(p, wk) + _round_bf16(p_new) * new_w) / den
    gs = jax.nn.sigmoid(gate_ref[...])
    o_nsa_ref[...] = gs[:, 0:1] * o_c + gs[:, 1:2] * o_s + gs[:, 2:3] * o_w
    kw_d = DIFF_HEADS * 2 * DIFF_QK
    qd = (qd_ref[...] * QK_SCALE).astype(BF16)
    def page_cat(j, first):
        parts = [diff_refs[j][pl.ds(first + h, PAGE_SIZE, stride=2 * DIFF_HEADS), :] for h in range(DIFF_HEADS)]
        return jnp.concatenate(parts, axis=1).astype(BF16)

    s_parts = [_dot_nt(qd, page_cat(j, 0)) for j in range(npg)]
    s = jnp.concatenate(s_parts, axis=1) + bias_d_ref[...]
    new_d = _round_bf16(dkv_new_ref[...])
    s_new = jnp.sum(qd.astype(F32) * new_d[:, 0:kw_d], axis=1, keepdims=True) + bnew_ref[:, 1:2]
    m = jnp.maximum(jnp.max(s, axis=1, keepdims=True), s_new)
    pf = jnp.exp(s - m)
    p_new = jnp.exp(s_new - m)
    den = jnp.sum(pf, axis=1, keepdims=True) + p_new
    p = pf.astype(BF16)
    acc = _round_bf16(p_new) * new_d[:, kw_d:]
    for j in range(npg):
        acc = acc + _dot(p[:, j * PAGE_SIZE:(j + 1) * PAGE_SIZE], page_cat(j, DIFF_HEADS))
    a = acc / den
    lam, lam_init = _lambda_value(lam_ref, cst_ref)
    d = a[0:DIFF_HEADS] - lam * a[DIFF_HEADS:]
    own = (lax.broadcasted_iota(jnp.int32, d.shape, 1) // DIFF_VD) == lax.broadcasted_iota(jnp.int32, d.shape, 0)
    d = jnp.where(own, d, 0.0)
    ms = jnp.sum(d * d, axis=1, keepdims=True) * (1.0 / DIFF_VD)
    y = ((d * lax.rsqrt(ms + EPS)) * sub_ref[...]) * (1.0 - lam_init)
    o_diff_ref[...] = jnp.sum(y, axis=0, keepdims=True)


def sample_mixer(page_ids, slc_pages, diff_pages, qs, qd, gate8, kcvc, slc_new, win_new, dkv_new, win_state,
                 bias_c, bias_s, bias_w, bias_d, bnew, expand, lam_vecs, sub_row, cst, npg, layer):
    nbatch = qs.shape[0]
    nb_past = npg * PAGE_SIZE // CMP_BLOCK
    n_top = min(N_SELECT - 1, nb_past)

    def page_spec(pages, j):
        return pl.BlockSpec((None,) + pages.shape[1:], lambda i, pt: (pt[i * npg + j], 0, 0))

    def per_b(arr):
        return pl.BlockSpec((None,) + arr.shape[1:], lambda i, pt: (i,) + (0,) * (arr.ndim - 1))

    def const(arr):
        return pl.BlockSpec(arr.shape, lambda i, pt: (0,) * arr.ndim)

    per_b_in = [qs, qd, gate8, kcvc, slc_new, win_new, dkv_new]
    const_in = [bias_c, bias_s, bias_w, bias_d, bnew, expand, lam_vecs, sub_row, cst]
    win_spec = pl.BlockSpec((None, None) + win_state.shape[2:], lambda i, pt: (layer, i, 0, 0))
    per_b_in_specs = [per_b(a) for a in per_b_in] + [win_spec]
    per_b_in = per_b_in + [win_state]
    grid_spec = pltpu.PrefetchScalarGridSpec(
        num_scalar_prefetch=1,
        grid=(nbatch,),
        in_specs=([page_spec(slc_pages, j) for j in range(npg)]
                  + [page_spec(diff_pages, j) for j in range(npg)]
                  + per_b_in_specs + [const(a) for a in const_in]),
        out_specs=[pl.BlockSpec((None, NSA_HEADS, 2 * LANES), lambda i, pt: (i, 0, 0)),
                   pl.BlockSpec((None, 1, DIFF_W), lambda i, pt: (i, 0, 0))],
    )
    return pl.pallas_call(
        functools.partial(_sample_kernel, npg=npg, n_top=n_top, nb_past=nb_past),
        grid_spec=grid_spec,
        out_shape=[jax.ShapeDtypeStruct((nbatch, NSA_HEADS, 2 * LANES), F32),
                   jax.ShapeDtypeStruct((nbatch, 1, DIFF_W), F32)],
        compiler_params=_cparams("arbitrary"),
        name="sample_mixer",
    )(page_ids, *([slc_pages] * npg), *([diff_pages] * npg), *per_b_in, *const_in)


def _t5_bucket(dist):
    n = jnp.maximum(dist, 0)
    max_exact = N_BUCKETS // 2
    nf = jnp.maximum(n, 1).astype(F32)
    large = max_exact + (jnp.log(nf / max_exact) / math.log(MAX_DISTANCE / max_exact)
                         * (N_BUCKETS - max_exact)).astype(jnp.int32)
    large = jnp.minimum(large, N_BUCKETS - 1)
    return jnp.where(n < max_exact, n, large)


def _prompt_bias_tables(rel_bias):
    c = jnp.arange(TK, dtype=jnp.int32)[:, None]
    r = jnp.arange(TQ, dtype=jnp.int32)[None, :]
    near = jnp.stack([rel_bias[_t5_bucket(TQ * o + r - c)] for o in range(2)])
    near = jnp.transpose(near, (3, 0, 1, 2))
    far = rel_bias[N_BUCKETS - 1]
    k3 = jnp.arange(3, dtype=jnp.int32)[:, None]
    cmp_near = jnp.transpose(rel_bias[_t5_bucket(r + (CMP_BLOCK + 1) - CMP_BLOCK * k3)], (2, 0, 1))
    near = near - far[:, None, None, None]
    cmp_near = cmp_near - far[:, None, None]
    causal_add = jnp.where(c <= r, 0.0, NEG)
    old_add = jnp.where(r < c, 0.0, NEG)

    def lanes(per_head):
        x = per_head.reshape((NSA_KV_HEADS, NSA_GROUP) + per_head.shape[1:])
        return jnp.transpose(x, (0, 2, 1, 3)).reshape(NSA_KV_HEADS, per_head.shape[1], NSA_GROUP * TQ)

    near_n = near[:NSA_HEADS]
    two = jnp.concatenate([near_n[:, 1], near_n[:, 0] + causal_add], axis=1)
    near_b = lanes(two)
    mid = jnp.zeros((NSA_HEADS, (WIN_TILES - 3) * TK, TQ), F32)
    win = jnp.concatenate([jnp.broadcast_to(old_add, (NSA_HEADS, TK, TQ)), mid, two], axis=1)
    win_b = lanes(win)
    bcn = jnp.pad(lanes(cmp_near[:NSA_HEADS]), ((0, 0), (0, 5), (0, 0)))
    nd = near[NSA_HEADS:]
    two_d = jnp.concatenate([nd[:, 1], nd[:, 0] + causal_add], axis=1)
    near_d = jnp.concatenate([two_d, two_d], axis=-1)
    return near_b, win_b, bcn, near_d


def _sample_bias_tables(rel_bias, past_len, w_buf):
    nsa_tab = rel_bias[:, :NSA_HEADS]
    diff_tab = rel_bias[:, NSA_HEADS:]
    diff_tab8 = jnp.concatenate([diff_tab, diff_tab], axis=1)
    pos = past_len
    k = jnp.arange(past_len, dtype=jnp.int32)
    bias_s = nsa_tab[_t5_bucket(pos - k)].T
    bias_d = diff_tab8[_t5_bucket(pos - k)].T
    wpos = past_len - w_buf + jnp.arange(w_buf, dtype=jnp.int32)
    bias_w = nsa_tab[_t5_bucket(pos - wpos)].T
    nb = past_len // CMP_BLOCK
    blk = jnp.arange(nb, dtype=jnp.int32)
    bias_c = nsa_tab[_t5_bucket(pos - (blk * CMP_BLOCK + CMP_BLOCK - 1))].T
    bias_c = jnp.pad(bias_c, ((0, 0), (0, LANES - nb)), constant_values=NEG)
    bnew = jnp.stack([nsa_tab[0], diff_tab8[0]], axis=1)
    bnew = jnp.pad(bnew, ((0, 0), (0, LANES - 2)))
    expand = (jnp.arange(LANES, dtype=jnp.int32)[:, None] == (k[None, :] // CMP_BLOCK)).astype(BF16)
    return bias_c, bias_s, bias_w, bias_d, bnew, expand


def kernel(x_prompt, x_sample, cache_cmp_kv, cache_slc_kv, cache_diff_kv, state_win_kv, page_table,
           w_in, w_o, w_cmp1, w_cmp2, cmp_pe, lam_q1, lam_k1, lam_q2, lam_k2, diff_subln, rel_bias,
           g_attn_pre, g_attn_post, g_mlp_pre, g_mlp_post, w_up, w_down):
    depth = w_in.shape[0]
    b, t, dm = x_prompt.shape
    nbatch = x_sample.shape[0]
    n_pool = cache_cmp_kv.shape[1]
    npg = page_table.shape[1]
    past_len = npg * PAGE_SIZE
    w_buf = state_win_kv.shape[2]
    nqt = t // TQ
    nb = t // CMP_BLOCK
    assert t % TQ == 0 and x_sample.shape[1] == 1 and w_buf == WINDOW and past_len % CMP_BLOCK == 0

    g_lo, g_hi = NSA_W, NSA_W + N_GATES
    proj_splits = (NSA_W, 2 * KV_W, 2 * KV_W, 2 * KV_W, DIFF_HEADS * 2 * DIFF_QK, 2 * DIFF_W, LANES)

    near_b, win_b, bcn, near_d = _prompt_bias_tables(rel_bias)
    bias_c, bias_s, bias_w, bias_d, bnew, expand = _sample_bias_tables(rel_bias, past_len, w_buf)

    def feature_major(c):
        return jnp.transpose(c, (0, 1, 3, 4, 5, 2)).reshape(depth * n_pool, 2 * KV_W, PAGE_SIZE)

    cmp_pages = feature_major(cache_cmp_kv)
    slc_pages = feature_major(cache_slc_kv)
    diff_pages = cache_diff_kv.reshape(depth * n_pool, PAGE_SIZE * 2 * DIFF_HEADS, DIFF_VD)
    win_state = jnp.transpose(state_win_kv, (0, 1, 3, 4, 5, 2)).reshape(depth, nbatch, 2 * KV_W, w_buf)

    xp = x_prompt.reshape(b * t, dm)
    xs = x_sample.reshape(nbatch, dm)
    p_states = ([], [], [], [])
    s_states = ([], [], [], [])

    for l in range(depth):
        lam_init = 0.8 - 0.6 * math.exp(-0.3 * l)
        w_l = w_in[l]
        w_proj = jnp.concatenate(
            [w_l[:, :g_lo], w_l[:, g_hi:], w_l[:, g_lo:g_hi], jnp.zeros((dm, LANES - N_GATES), F32)],
            axis=1).astype(BF16)
        w_o_l = w_o[l].astype(BF16)
        w_up_l = w_up[l].astype(BF16)
        w_down_l = w_down[l].astype(BF16)
        c0 = NSA_W + N_GATES
        dq0 = c0 + 6 * KV_W
        w_nn = jnp.concatenate([w_l[:, c0 + 2 * KV_W:c0 + 3 * KV_W], w_l[:, c0 + 4 * KV_W:c0 + 5 * KV_W],
                                w_l[:, dq0 + DIFF_W:]], axis=1).astype(BF16)
        w_lt = w_l.T
        w_nt = jnp.concatenate([w_lt[:NSA_W], w_lt[NSA_W:c0], jnp.zeros((32 - N_GATES, dm), F32),
                                w_lt[c0:dq0 + DIFF_W], w_lt[dq0 + 2 * DIFF_W:]], axis=0).astype(BF16)
        wt = jnp.transpose(w_cmp1[l].reshape(2, CMP_BLOCK, NSA_HD // 2, 2, -1), (0, 2, 3, 1, 4))
        w1d = jnp.concatenate([wt[:, :, 0], wt[:, :, 0], wt[:, :, 1], wt[:, :, 1]], axis=2).astype(BF16)
        w2 = w_cmp2[l].astype(BF16)
        pe = cmp_pe[l]
        pe_t = jnp.concatenate([jnp.transpose(pe, (0, 2, 1))] * 2, axis=2)
        lam_vecs = jnp.stack([lam_q1[l], lam_k1[l], lam_q2[l], lam_k2[l]])
        cst = jnp.full((1, LANES), lam_init, F32)
        sub = diff_subln[l]

        (ks, kw, kd, dkv, wq_n, gt, cmp_t, slc_t, win_t, vst, vwt, wq_d, vd_t) = proj_prompt(
            xp, g_attn_pre[l][None], w_nn, w_nt, b, t)
        n_pages_p = b * t // PAGE_SIZE
        npg_p = min(64, n_pages_p)
        comp = compress_pages(cmp_t, jnp.arange(n_pages_p, dtype=jnp.int32), pe_t, w1d, w2, npg_p,
                              name="compress_prompt")
        comp = comp.reshape(n_pages_p // npg_p, 2, NSA_KV_HEADS, 2, npg_p, NSA_HD)
        comp = jnp.transpose(comp, (1, 2, 0, 4, 3, 5)).reshape(2, NSA_KV_HEADS, b, nb, NSA_HD)
        kc = jnp.transpose(comp[0], (1, 2, 0, 3)).reshape(b, nb, KV_W).astype(BF16)
        vct = jnp.transpose(comp[1], (1, 0, 3, 2)).reshape(b, KV_W, nb).astype(BF16)
        o_nsa_t = nsa_prompt(wq_n, kc, vct, ks.reshape(b, nqt, TK, KV_W), vst, kw.reshape(b, nqt, TK, KV_W), vwt,
                             gt, near_b, win_b, bcn)
        o_diff_t = diff_prompt(wq_d, kd.reshape(b, DIFF_HEADS, nqt, TK, DIFF_VD), vd_t, near_d,
                               lam_vecs, sub[:, None], cst)
        xp = oproj_t(o_nsa_t, o_diff_t, w_o_l, g_attn_post[l][None], xp)
        (hid,) = rms_mm(xp, g_mlp_pre[l][None], w_up_l, (w_up_l.shape[1],), BF16, act=True, name="mlp_up_prompt")
        xp = mm_rms_res(hid, w_down_l, g_mlp_post[l][None], xp, name="mlp_down_prompt")

        def token_major(a_t):
            return jnp.transpose(a_t.reshape(b, 2, NSA_KV_HEADS, NSA_HD, a_t.shape[-1]), (0, 4, 1, 2, 3))

        p_states[0].append(token_major(cmp_t))
        p_states[1].append(token_major(slc_t))
        p_states[2].append(dkv.reshape(b, t, 2, DIFF_HEADS, DIFF_VD))
        w_keep = min(WINDOW, t)
        p_states[3].append(token_major(win_t[:, :, t - w_keep:]))

        q, cmp, slc, win, dq, dkv, gates = rms_mm(xs, g_attn_pre[l][None], w_proj, proj_splits, F32,
                                                  name="proj_sample")
        page_ids = (page_table + l * n_pool).reshape(-1).astype(jnp.int32)
        npg_step = 4 * npg
        comp = compress_pages(cmp_pages, page_ids, pe_t, w1d, w2, npg_step, name="compress_pages")
        nb_past = past_len // CMP_BLOCK
        comp = comp.reshape(nbatch // 4, 2, NSA_KV_HEADS, 2, 4, npg, NSA_HD)
        kcvc = jnp.transpose(comp, (0, 4, 1, 2, 6, 5, 3)).reshape(nbatch, 2 * KV_W, nb_past)
        kcvc = jnp.pad(kcvc, ((0, 0), (0, 0), (0, LANES - nb_past)))

        q4 = q.reshape(nbatch, NSA_KV_HEADS, NSA_GROUP, NSA_HD)
        zq4 = jnp.zeros_like(q4[:, 0])
        qs = jnp.concatenate([jnp.concatenate([q4[:, 0], zq4, zq4, zq4], axis=-1),
                              jnp.concatenate([zq4, q4[:, 1], zq4, zq4], axis=-1)], axis=1)
        dq4 = jnp.transpose(dq.reshape(nbatch, DIFF_HEADS, 2, DIFF_QK), (0, 2, 1, 3))
        eye_h = jnp.eye(DIFF_HEADS, dtype=F32)[None, :, :, None, None]
        eye_m = jnp.eye(2, dtype=F32)[:, None, None, :, None]
        qd = (dq4[:, :, :, None, None, :] * (eye_h * eye_m)).reshape(nbatch, 2 * DIFF_HEADS, DIFF_W)
        gate8 = jnp.pad(gates[:, :N_GATES].reshape(nbatch, NSA_HEADS, 3), ((0, 0), (0, 0), (0, LANES - 3)))
        sub_row = jnp.tile(sub, DIFF_HEADS)[None]
        o_nsa8, o_diff = sample_mixer(
            page_ids, slc_pages, diff_pages, qs, qd, gate8, kcvc,
            slc[:, None], win[:, None], dkv[:, None], win_state,
            bias_c, bias_s, bias_w, bias_d, bnew, expand, lam_vecs, sub_row, cst, npg, l)
        o8 = o_nsa8.reshape(nbatch, NSA_KV_HEADS, NSA_GROUP, 4, NSA_HD)
        o_nsa = jnp.stack([o8[:, 0, :, 2], o8[:, 1, :, 3]], axis=1).reshape(nbatch, NSA_W)
        o_cat = jnp.concatenate([o_nsa, o_diff.reshape(nbatch, DIFF_W)], axis=1).astype(BF16)
        xs = mm_rms_res(o_cat, w_o_l, g_attn_post[l][None], xs, name="oproj_sample")
        (hid,) = rms_mm(xs, g_mlp_pre[l][None], w_up_l, (w_up_l.shape[1],), BF16, act=True, name="mlp_up_sample")
        xs = mm_rms_res(hid, w_down_l, g_mlp_post[l][None], xs, name="mlp_down_sample")

        s_states[0].append(cmp.reshape(nbatch, 1, 2, NSA_KV_HEADS, NSA_HD))
        s_states[1].append(slc.reshape(nbatch, 1, 2, NSA_KV_HEADS, NSA_HD))
        s_states[2].append(dkv.reshape(nbatch, 1, 2, DIFF_HEADS, DIFF_VD))
        win_new = win.reshape(nbatch, 1, 2, NSA_KV_HEADS, NSA_HD)
        s_states[3].append(win_new)

    return (xp.reshape(b, t, dm), xs.reshape(nbatch, 1, dm),
            jnp.stack(p_states[0]), jnp.stack(p_states[1]), jnp.stack(p_states[2]), jnp.stack(p_states[3]),
            jnp.stack(s_states[0]), jnp.stack(s_states[1]), jnp.stack(s_states[2]),
            jnp.concatenate([state_win_kv[:, :, 1:], jnp.stack(s_states[3])], axis=2))
```
